```python
import math
import jax
import jax.numpy as jnp
from jax import lax
import numpy as np

D_MODEL = 4096
BATCH = 2
SEQ = 4096
DEPTH = 2

GRID_W = 64
CTX_LEN = 256
MIX_W = D_MODEL // 2
NORM_EPS = 1e-6
NEG_INF = -1e30
ROPE_BASE = 10000.0
DN_HEADS = 16
DN_HD = MIX_W // DN_HEADS
DN_CONV = 5
DN_CHUNK = 64
RW_HD = 64
RW_HEADS = MIX_W // RW_HD
RW_DECAY_LORA = 128
RW_A_LORA = 128
RW_GATE_LORA = 448
RW_LN_EPS = 64e-5
NA_HEADS = 16
NA_HD = MIX_W // NA_HEADS
NA_WIN_R = 8
NA_WIN_C = 16
HY_CH = MIX_W
HY_SHORT = 3
HY_EMB = 33
HY_FFN = 64
HY_TARGET = 1e-2
HY_MAX_DECAY = math.log(HY_TARGET) / 0.3
HY_MIN_DECAY = math.log(HY_TARGET) / 1.5
N_EXPERTS = 16
N_GROUPS = 4
TOPK_GROUP = 1
TOP_K = 2
D_FF = 1024
P_A = 4 * MIX_W + 4 * DN_HEADS
P_B = 3 * MIX_W + 2 * RW_DECAY_LORA + 2 * RW_A_LORA + RW_GATE_LORA
P_AB = P_A + P_B
P_CD = 3 * MIX_W + 3 * HY_CH
N_AB = (DEPTH + 1) // 2
N_CD = DEPTH // 2
F32 = jnp.float32

kernel_name = 'hybrid_deltanet_rwkv7_natten_hyena_moe_dit'


def rmsnorm(x, g):
    xf = x.astype(F32)
    y = xf * lax.rsqrt(jnp.mean(xf * xf, -1, keepdims=True) + NORM_EPS) * g.astype(F32)
    return y.astype(x.dtype)


def l2norm(t):
    t = t.astype(F32)
    return t * lax.rsqrt(jnp.sum(t * t, -1, keepdims=True) + 1e-6)


def _heads(t, h, d):
    return t.reshape(*t.shape[:-1], h, d)


def dwconv(x, w):
    k, c = w.shape
    return lax.conv_general_dilated(x, w[:, None, :].astype(x.dtype), window_strides=(1,),
                                    padding=[(k // 2, k // 2)], dimension_numbers=('NWC', 'WIO', 'NWC'),
                                    feature_group_count=c)


def centred_shift(p):
    prev = jnp.pad(p, ((0, 0), (1, 0), (0, 0)))[:, :-1]
    nxt = jnp.pad(p, ((0, 0), (0, 1), (0, 0)))[:, 1:]
    return 0.5 * (prev + nxt)


def axial_rope(t):
    L, d = t.shape[1], t.shape[-1]
    half = d // 2
    nf = half // 2
    pos = jnp.arange(L)
    inv = ROPE_BASE ** (-jnp.arange(nf, dtype=F32) / nf)

    def rot(u, p):
        ang = p.astype(F32)[:, None] * inv[None, :]
        cos = jnp.cos(ang)[None, :, None, :]
        sin = jnp.sin(ang)[None, :, None, :]
        u1, u2 = u[..., :nf], u[..., nf:]
        return jnp.concatenate([u1 * cos - u2 * sin, u1 * sin + u2 * cos], -1)

    return jnp.concatenate([rot(t[..., :half], pos // GRID_W), rot(t[..., half:], pos % GRID_W)], -1)


def chunk_gated_delta(q, k, v, g, beta, s0):
    B, H, L, K = k.shape
    V = v.shape[-1]
    C = DN_CHUNK
    n = L // C
    q, k, v = (t.reshape(B, H, n, C, t.shape[-1]) for t in (q, k, v))
    g = jnp.cumsum(g.reshape(B, H, n, C), axis=-1)
    beta = beta.reshape(B, H, n, C)
    incl = jnp.tril(jnp.ones((C, C), bool))
    strict = jnp.tril(jnp.ones((C, C), bool), -1)
    gdiff = g[..., :, None] - g[..., None, :]
    decay = jnp.where(incl, jnp.exp(jnp.where(incl, gdiff, 0.0)), 0.0)
    kb = k * beta[..., None]
    m = jnp.where(strict, jnp.einsum('bhnik,bhnjk->bhnij', kb, k) * decay, 0.0)
    a = m + jnp.eye(C, dtype=m.dtype)
    rhs = jnp.concatenate([v * beta[..., None], kb * jnp.exp(g)[..., None]], -1)
    sol = lax.linalg.triangular_solve(a, rhs, left_side=True, lower=True, unit_diagonal=True)
    u, w = sol[..., :V], sol[..., V:]
    qk = jnp.where(incl, jnp.einsum('bhnik,bhnjk->bhnij', q, k) * decay, 0.0)
    qg = q * jnp.exp(g)[..., None]
    kd = k * jnp.exp(g[..., -1:] - g)[..., None]
    glast = jnp.exp(g[..., -1])

    def step(s, xs):
        qg_i, kd_i, u_i, w_i, qk_i, gl_i = xs
        v_new = u_i - jnp.einsum('bhck,bhkv->bhcv', w_i, s)
        o = jnp.einsum('bhck,bhkv->bhcv', qg_i, s) + jnp.einsum('bhij,bhjv->bhiv', qk_i, v_new)
        s = s * gl_i[..., None, None] + jnp.einsum('bhck,bhcv->bhkv', kd_i, v_new)
        return s, o

    xs = tuple(jnp.moveaxis(t, 2, 0) for t in (qg, kd, u, w, qk, glast))
    s, o = lax.scan(step, s0, xs)
    return jnp.moveaxis(o, 0, 2).reshape(B, H, L, V), s


def dn_prep(pa, conv_w, a_log, dt_bias, latent):
    B, L, _ = pa.shape
    qkv = jax.nn.silu(dwconv(pa[..., :3 * MIX_W], conv_w))
    q, k, v = (_heads(t, DN_HEADS, DN_HD).astype(F32) for t in jnp.split(qkv, 3, axis=-1))
    q, k = l2norm(q), l2norm(k)
    if latent:
        q, k = axial_rope(q), axial_rope(k)
    q = q * DN_HD ** -0.5
    z = pa[..., 3 * MIX_W:4 * MIX_W]
    bg = pa[..., 4 * MIX_W:].astype(F32).reshape(B, L, 4, DN_HEADS)
    beta = jax.nn.sigmoid(bg[:, :, :2])
    g = -jnp.exp(a_log.astype(F32)) * jax.nn.softplus(bg[:, :, 2:] + dt_bias)
    bhl = lambda t: jnp.transpose(t, (0, 2, 1, 3))
    per_dir = lambda t: jnp.transpose(t, (2, 0, 3, 1))
    return (bhl(q), bhl(k), bhl(v), per_dir(beta), per_dir(g)), z


def dn_bidir(q, k, v, beta, g, s0f, s0b):
    flip = lambda t: jnp.flip(t, axis=2)
    of, sf = chunk_gated_delta(q, k, v, g[0], beta[0], s0f)
    ob, sb = chunk_gated_delta(flip(q), flip(k), flip(v), flip(g[1]), flip(beta[1]), s0b)
    return of + flip(ob), sf, sb


def dn_out(o, z, norm_w):
    B, H, L, V = o.shape
    o = jnp.transpose(o, (0, 2, 1, 3))
    o = o * lax.rsqrt(jnp.mean(o * o, -1, keepdims=True) + NORM_EPS) * norm_w.astype(F32)
    return (o * jax.nn.silu(z.astype(F32)).reshape(B, L, H, V)).reshape(B, L, H * V)


def rwkv7_scan(r, decay, k, v, kk, a, s0):
    def step(s, xs):
        r_t, w_t, k_t, v_t, kk_t, a_t = xs
        sa = jnp.einsum('bhvk,bhk->bhv', s, kk_t)
        s = (s * w_t[:, :, None, :] - sa[..., None] * (kk_t * a_t)[:, :, None, :]
             + v_t[..., None] * k_t[:, :, None, :])
        return s, jnp.einsum('bhvk,bhk->bhv', s, r_t)

    xs = tuple(jnp.moveaxis(t, 1, 0) for t in (r, decay, k, v, kk, a))
    s, y = lax.scan(step, s0, xs)
    return jnp.moveaxis(y, 0, 1), s


def rw_prep(pb, mu, w0, w2, a0, a2, g2, k_k, k_a):
    pb = pb.astype(F32)
    pb = pb + (centred_shift(pb) - pb) * mu
    B, L, _ = pb.shape
    r, k, v = pb[..., :MIX_W], pb[..., MIX_W:2 * MIX_W], pb[..., 2 * MIX_W:3 * MIX_W]
    o1 = 3 * MIX_W
    o2 = o1 + 2 * RW_DECAY_LORA
    o3 = o2 + 2 * RW_A_LORA
    wl = pb[..., o1:o2].reshape(B, L, 2, RW_DECAY_LORA)
    al = pb[..., o2:o3].reshape(B, L, 2, RW_A_LORA)
    gl = pb[..., o3:]
    w = -jax.nn.softplus(-(w0 + jnp.einsum('bldr,drc->bldc', jnp.tanh(wl), w2))) - 0.5
    decay = jnp.exp(-jnp.exp(w))
    a = jax.nn.sigmoid(a0 + jnp.einsum('bldr,drc->bldc', al, a2))
    gate = jax.nn.sigmoid(gl) @ g2
    kk = l2norm(_heads(k * k_k, RW_HEADS, RW_HD))
    kd = k[:, :, None, :] * (1.0 + (a - 1.0) * k_a)
    hd = lambda t: _heads(t, RW_HEADS, RW_HD)
    hr, hk, hv = hd(r), hd(k), hd(v)
    return (hr, hv, kk, hd(decay), hd(a), hd(kd)), (hr, hk, hv, gate)


def rw_bidir(r, v, kk, decay, a, kd, s0f, s0b):
    yf, sf = rwkv7_scan(r, decay[:, :, 0], kd[:, :, 0], v, kk, a[:, :, 0], s0f)
    fl = lambda t: jnp.flip(t, axis=1)
    yb, sb = rwkv7_scan(fl(r), fl(decay[:, :, 1]), fl(kd[:, :, 1]), fl(v), fl(kk), fl(a[:, :, 1]), s0b)
    return yf + fl(yb), sf, sb


def rw_out(y, r, k, v, gate, r_k, ln_w, ln_b):
    B, L, H, N = y.shape
    mu = jnp.mean(y, -1, keepdims=True)
    var = jnp.mean(jnp.square(y - mu), -1, keepdims=True)
    yn = ((y - mu) * lax.rsqrt(var + RW_LN_EPS)).reshape(B, L, H * N) * ln_w + ln_b
    bonus = (jnp.sum(r * k * r_k, -1, keepdims=True) * v).reshape(B, L, H * N)
    return (yn + bonus) * gate


def mixer_ab(hc, hl, w_in, dn_conv_w, dn_a_log, dn_dt_bias, dn_norm_w,
             rw_mu, rw_w0, rw_w2, rw_a0, rw_a2, rw_g2, rw_k_k, rw_k_a, rw_r_k, rw_ln_w, rw_ln_b,
             w_out, need_ctx):
    B = hl.shape[0]
    pc, pl = hc @ w_in, hl @ w_in
    dc, zc = dn_prep(pc[..., :P_A], dn_conv_w, dn_a_log, dn_dt_bias, latent=False)
    dl, zl = dn_prep(pl[..., :P_A], dn_conv_w, dn_a_log, dn_dt_bias, latent=True)
    s0 = jnp.zeros((B, DN_HEADS, DN_HD, DN_HD), F32)
    oc, sf, sb = dn_bidir(*dc, s0, s0)
    ol, _, _ = dn_bidir(*dl, sf, sb)
    rwp = (rw_mu, rw_w0, rw_w2, rw_a0, rw_a2, rw_g2, rw_k_k, rw_k_a)
    rc_scan, rc_out = rw_prep(pc[..., P_A:], *rwp)
    rl_scan, rl_out = rw_prep(pl[..., P_A:], *rwp)
    r0 = jnp.zeros((B, RW_HEADS, RW_HD, RW_HD), F32)
    yc, tf, tb = rw_bidir(*rc_scan, r0, r0)
    yl, _, _ = rw_bidir(*rl_scan, tf, tb)

    def merge(o, z, y, rout):
        cat = jnp.concatenate([dn_out(o, z, dn_norm_w), rw_out(y, *rout, rw_r_k, rw_ln_w, rw_ln_b)], -1)
        return cat.astype(hl.dtype) @ w_out

    lat = merge(ol, zl, yl, rl_out)
    if not need_ctx:
        return lat, None
    return lat, merge(oc, zc, yc, rc_out)


def neighbourhood_attention(q, k, v, kc, vc, rpb):
    B, S, H, dh = q.shape
    rows = S // GRID_W
    wr = min(NA_WIN_R, rows)
    wc = NA_WIN_C
    grid = lambda t: t.reshape(B, rows, GRID_W, H, dh)
    qg, kg, vg = grid(q * dh ** -0.5), grid(k), grid(v)
    qrow = jnp.arange(rows)
    row_start = jnp.clip(qrow - wr // 2, 0, rows - wr)
    col = jnp.arange(GRID_W)
    col_start = jnp.clip(col - wc // 2, 0, GRID_W - wc)
    col_ok = (col[None, :] >= col_start[:, None]) & (col[None, :] < col_start[:, None] + wc)
    dc_idx = jnp.clip(col[None, :] - col[:, None] + NA_WIN_C - 1, 0, 2 * NA_WIN_C - 2)

    def one_row(args):
        r, rs = args
        q_r = lax.dynamic_index_in_dim(qg, r, axis=1, keepdims=False)
        k_r = lax.dynamic_slice_in_dim(kg, rs, wr, axis=1)
        v_r = lax.dynamic_slice_in_dim(vg, rs, wr, axis=1)
        dr_idx = rs + jnp.arange(wr) - r + NA_WIN_R - 1
        bias = rpb[:, dr_idx][:, :, dc_idx]
        s_lat = (jnp.einsum('bqhd,bwkhd->bhqwk', q_r, k_r).astype(F32)
                 + jnp.transpose(bias, (0, 2, 1, 3)).astype(F32)[None])
        s_lat = jnp.where(col_ok[:, None, :], s_lat, NEG_INF)
        s_ctx = jnp.einsum('bqhd,bchd->bhqc', q_r, kc).astype(F32)
        nl = wr * GRID_W
        s = jnp.concatenate([s_lat.reshape(B, H, GRID_W, nl), s_ctx], -1)
        p = jax.nn.softmax(s, axis=-1).astype(v.dtype)
        return (jnp.einsum('bhqk,bkhd->bqhd', p[..., :nl], v_r.reshape(B, nl, H, dh))
                + jnp.einsum('bhqc,bchd->bqhd', p[..., nl:], vc))

    out = lax.map(one_row, (qrow, row_start))
    return jnp.transpose(out, (1, 0, 2, 3, 4)).reshape(B, S, H, dh)


def context_attention(q, k, v):
    s = jnp.einsum('bqhd,bkhd->bhqk', q, k).astype(F32) * (q.shape[-1] ** -0.5)
    p = jax.nn.softmax(s, axis=-1).astype(v.dtype)
    return jnp.einsum('bhqk,bkhd->bqhd', p, v)


def hyena_filters(L, w1, b1, freq, w2, b2, w3, b3, w4):
    t = jnp.linspace(0.0, 1.0, L, dtype=F32)
    bands = (HY_EMB - 1) // 2
    wpos = 2 * math.pi * jnp.arange(L, dtype=F32) / L
    fb = jnp.linspace(1e-4, bands - 1, bands, dtype=F32)
    ang = wpos[:, None] * fb[None, :]
    z = jnp.concatenate([t[:, None], jnp.cos(ang), -jnp.sin(ang)], axis=-1)
    h = jnp.sin(freq * (z @ w1 + b1))
    h = jnp.sin(freq * (h @ w2 + b2))
    h = jnp.sin(freq * (h @ w3 + b3))
    h = (h @ w4).astype(F32).reshape(L, 2, HY_CH)
    deltas = jnp.abs(jnp.linspace(HY_MIN_DECAY, HY_MAX_DECAY, HY_CH, dtype=F32))
    window = jnp.exp(-t[:, None] * deltas[None, :])
    return h[:, 0] * window, h[:, 1] * window


def bidir_long_conv(u, h_f, h_b, skip):
    B, L, C = u.shape
    filt2l = jnp.concatenate([h_f, jnp.zeros((1, C), F32), h_b[1:][::-1]], axis=0)
    kf = jnp.fft.rfft(filt2l, n=2 * L, axis=0)
    uf = jnp.fft.rfft(u, n=2 * L, axis=1)
    y = jnp.fft.irfft(uf * kf[None], n=2 * L, axis=1)[:, :L]
    return y + u * skip


def hyena(p, conv_w, conv_b, w1, b1, freq, w2, b2, w3, b3, w4, skip):
    L = p.shape[1]
    u = (dwconv(p, conv_w) + conv_b).astype(F32)
    x0, x1, v = jnp.split(u, 3, axis=-1)
    h_f, h_b = hyena_filters(L, w1, b1, freq, w2, b2, w3, b3, w4)
    return x0 * bidir_long_conv(x1 * v, h_f, h_b, skip)


def mixer_cd(hc, hl, w_in, na_rpb, hy_conv_w, hy_conv_b, filt, hy_skip, w_out, need_ctx):
    B, S, _ = hl.shape
    Lc = hc.shape[1]
    pc, pl = hc @ w_in, hl @ w_in
    split_heads = lambda p: [_heads(t, NA_HEADS, NA_HD) for t in jnp.split(p[..., :3 * MIX_W], 3, axis=-1)]
    qc, kc, vc = split_heads(pc)
    ql, kl, vl = split_heads(pl)
    att_l = neighbourhood_attention(ql, kl, vl, kc, vc, na_rpb).reshape(B, S, MIX_W)
    hy_l = hyena(pl[..., 3 * MIX_W:], hy_conv_w, hy_conv_b, *filt, hy_skip)
    lat = jnp.concatenate([att_l.astype(F32), hy_l], -1).astype(hl.dtype) @ w_out
    if not need_ctx:
        return lat, None
    att_c = context_attention(qc, kc, vc).reshape(B, Lc, MIX_W)
    hy_c = hyena(pc[..., 3 * MIX_W:], hy_conv_w, hy_conv_b, *filt, hy_skip)
    return lat, jnp.concatenate([att_c.astype(F32), hy_c], -1).astype(hc.dtype) @ w_out


def moe(h, router_w, router_bias, w_gate, w_up, w_down):
    B, L, D = h.shape
    t = h.reshape(B * L, D)
    n = t.shape[0]
    s = jax.nn.sigmoid((t @ router_w).astype(F32))
    sel = s + router_bias.astype(F32)
    grp = lax.top_k(sel.reshape(n, N_GROUPS, N_EXPERTS // N_GROUPS), TOP_K)[0].sum(-1)
    gidx = lax.top_k(grp, TOPK_GROUP)[1]
    gmask = jax.nn.one_hot(gidx, N_GROUPS, dtype=F32).sum(1) > 0
    emask = jnp.repeat(gmask, N_EXPERTS // N_GROUPS, axis=1)
    eidx = lax.top_k(jnp.where(emask, sel, NEG_INF), TOP_K)[1]
    wsel = jnp.take_along_axis(s, eidx, axis=1)
    wsel = wsel / jnp.sum(wsel, -1, keepdims=True)
    gates = jnp.sum(jax.nn.one_hot(eidx, N_EXPERTS, dtype=F32) * wsel[..., None], axis=1)
    hg = jnp.einsum('nd,edf->enf', t, w_gate)
    hu = jnp.einsum('nd,edf->enf', t, w_up)
    act = jax.nn.silu(hg) * hu * gates.T[:, :, None].astype(hg.dtype)
    return jnp.einsum('enf,efd->nd', act, w_down).reshape(B, L, D)


def setup_inputs(seed: int = 0) -> dict:
    key = jax.random.key(seed)
    ks = iter(jax.random.split(key, 64))
    D = D_MODEL

    def nrm(shape, scale):
        return jax.random.normal(next(ks), shape, F32) * scale

    def unif(shape, lo, hi):
        return jax.random.uniform(next(ks), shape, F32, lo, hi)

    dt = jnp.exp(unif((N_AB, 2, DN_HEADS), math.log(1e-3), math.log(1e-1)))
    return {
        'x': nrm((BATCH, SEQ, D), 1.0),
        'c': nrm((BATCH, D), 1.0),
        'ctx': nrm((BATCH, CTX_LEN, D), 1.0),
        'c_ctx': nrm((D,), 1.0),
        'ada_w': nrm((DEPTH, D, 6 * D), 0.5 * D ** -0.5),
        'ada_b': nrm((DEPTH, 6 * D), 0.02),
        'norm1_g': 1.0 + nrm((DEPTH, D), 0.02),
        'norm2_g': 1.0 + nrm((DEPTH, D), 0.02),
        'final_g': 1.0 + nrm((D,), 0.02),
        'ab_w_in': nrm((N_AB, D, P_AB), D ** -0.5),
        'dn_conv_w': nrm((N_AB, DN_CONV, 3 * MIX_W), DN_CONV ** -0.5),
        'dn_a_log': jnp.log(unif((N_AB, 2, DN_HEADS), 1.0, 16.0)),
        'dn_dt_bias': dt + jnp.log(-jnp.expm1(-dt)),
        'dn_norm_w': 1.0 + nrm((N_AB, DN_HD), 0.02),
        'rw_mu': unif((N_AB, P_B), 0.0, 1.0),
        'rw_w0': unif((N_AB, 2, MIX_W), -6.0, -1.0),
        'rw_w2': nrm((N_AB, 2, RW_DECAY_LORA, MIX_W), 0.5 * RW_DECAY_LORA ** -0.5),
        'rw_a0': nrm((N_AB, 2, MIX_W), 0.1),
        'rw_a2': nrm((N_AB, 2, RW_A_LORA, MIX_W), 0.5 * RW_A_LORA ** -0.5),
        'rw_g2': nrm((N_AB, RW_GATE_LORA, MIX_W), RW_GATE_LORA ** -0.5),
        'rw_k_k': 1.0 + nrm((N_AB, MIX_W), 0.02),
        'rw_k_a': 1.0 + nrm((N_AB, MIX_W), 0.02),
        'rw_r_k': nrm((N_AB, RW_HEADS, RW_HD), 0.1),
        'rw_ln_w': 1.0 + nrm((N_AB, MIX_W), 0.02),
        'rw_ln_b': nrm((N_AB, MIX_W), 0.02),
        'ab_w_out': nrm((N_AB, 2 * MIX_W, D), (2 * MIX_W) ** -0.5),
        'cd_w_in': nrm((N_CD, D, P_CD), D ** -0.5),
        'na_rpb': nrm((N_CD, NA_HEADS, 2 * NA_WIN_R - 1, 2 * NA_WIN_C - 1), 0.02),
        'hy_conv_w': nrm((N_CD, HY_SHORT, 3 * HY_CH), HY_SHORT ** -0.5),
        'hy_conv_b': nrm((N_CD, 3 * HY_CH), 0.02),
        'hy_w1': nrm((N_CD, HY_EMB, HY_FFN), HY_EMB ** -0.5),
        'hy_b1': nrm((N_CD, HY_FFN), 0.02),
        'hy_freq': 1.0 + nrm((N_CD, HY_FFN), 0.02),
        'hy_w2': nrm((N_CD, HY_FFN, HY_FFN), HY_FFN ** -0.5),
        'hy_b2': nrm((N_CD, HY_FFN), 0.02),
        'hy_w3': nrm((N_CD, HY_FFN, HY_FFN), HY_FFN ** -0.5),
        'hy_b3': nrm((N_CD, HY_FFN), 0.02),
        'hy_w4': nrm((N_CD, HY_FFN, 2 * HY_CH), 0.05 * HY_FFN ** -0.5),
        'hy_skip': nrm((N_CD, HY_CH), 0.1),
        'cd_w_out': nrm((N_CD, 2 * MIX_W, D), (2 * MIX_W) ** -0.5),
        'router_w': nrm((D, N_EXPERTS), D ** -0.5),
        'router_bias': nrm((N_EXPERTS,), 0.01),
        'moe_w_gate': nrm((DEPTH, N_EXPERTS, D, D_FF), D ** -0.5),
        'moe_w_up': nrm((DEPTH, N_EXPERTS, D, D_FF), D ** -0.5),
        'moe_w_down': nrm((DEPTH, N_EXPERTS, D_FF, D), D_FF ** -0.5),
    }


def reference(x, c, ctx, c_ctx, ada_w, ada_b, norm1_g, norm2_g, final_g,
              ab_w_in, dn_conv_w, dn_a_log, dn_dt_bias, dn_norm_w,
              rw_mu, rw_w0, rw_w2, rw_a0, rw_a2, rw_g2, rw_k_k, rw_k_a, rw_r_k, rw_ln_w, rw_ln_b, ab_w_out,
              cd_w_in, na_rpb, hy_conv_w, hy_conv_b, hy_w1, hy_b1, hy_freq, hy_w2, hy_b2, hy_w3, hy_b3, hy_w4,
              hy_skip, cd_w_out, router_w, router_bias, moe_w_gate, moe_w_up, moe_w_down):
    cx = ctx
    Lc = ctx.shape[1]
    silu_c = jax.nn.silu(c)
    silu_cc = jax.nn.silu(c_ctx)
    for l in range(DEPTH):
        last = l == DEPTH - 1
        i = l // 2
        sh1, sc1, g1, sh2, sc2, g2 = [t[:, None, :] for t in jnp.split(silu_c @ ada_w[l] + ada_b[l], 6, axis=-1)]
        sh1c, sc1c, g1c, sh2c, sc2c, g2c = jnp.split(silu_cc @ ada_w[l] + ada_b[l], 6, axis=-1)
        hl = rmsnorm(x, norm1_g[l]) * (1 + sc1) + sh1
        hc = rmsnorm(cx, norm1_g[l]) * (1 + sc1c) + sh1c
        if l % 2 == 0:
            ml, mc = mixer_ab(hc, hl, ab_w_in[i], dn_conv_w[i], dn_a_log[i], dn_dt_bias[i], dn_norm_w[i],
                              rw_mu[i], rw_w0[i], rw_w2[i], rw_a0[i], rw_a2[i], rw_g2[i], rw_k_k[i], rw_k_a[i],
                              rw_r_k[i], rw_ln_w[i], rw_ln_b[i], ab_w_out[i], need_ctx=not last)
        else:
            filt = (hy_w1[i], hy_b1[i], hy_freq[i], hy_w2[i], hy_b2[i], hy_w3[i], hy_b3[i], hy_w4[i])
            ml, mc = mixer_cd(hc, hl, cd_w_in[i], na_rpb[i], hy_conv_w[i], hy_conv_b[i], filt, hy_skip[i],
                              cd_w_out[i], need_ctx=not last)
        x = x + g1 * ml.astype(x.dtype)
        hl = rmsnorm(x, norm2_g[l]) * (1 + sc2) + sh2
        if last:
            x = x + g2 * moe(hl, router_w, router_bias, moe_w_gate[l], moe_w_up[l], moe_w_down[l])
        else:
            cx = cx + g1c * mc.astype(cx.dtype)
            hc = rmsnorm(cx, norm2_g[l]) * (1 + sc2c) + sh2c
            f = moe(jnp.concatenate([hc, hl], axis=1), router_w, router_bias,
                    moe_w_gate[l], moe_w_up[l], moe_w_down[l])
            cx = cx + g2c * f[:, :Lc]
            x = x + g2 * f[:, Lc:]
    return rmsnorm(x, final_g)
```

```python
import functools
import math

import jax
import jax.numpy as jnp
import numpy as np
from jax import lax
from jax.experimental import pallas as pl
from jax.experimental.pallas import tpu as pltpu

F32 = jnp.float32
BF16 = jnp.bfloat16

V7X_LANES = 128
V7X_SUBLANES = 8
V7X_MXU_DIM = 256
V7X_VMEM_LIMIT_BYTES = 56 * 1024 * 1024

GRID_W = 64
NORM_EPS = 1e-6
NEG_INF = -1e30
ROPE_BASE = 10000.0
DN_HEADS = 16
DN_CONV = 5
RW_HD = 64
RW_LN_EPS = 64e-5
NA_HEADS = 16
NA_WIN_R = 8
NA_WIN_C = 16
HY_EMB = 33
HY_TARGET = 1e-2
HY_MAX_DECAY = math.log(HY_TARGET) / 0.3
HY_MIN_DECAY = math.log(HY_TARGET) / 1.5
N_EXPERTS = 16
N_GROUPS = 4

ROW_TILE = 256
SCAN_CHUNK = 64


def _params(semantics, vmem=V7X_VMEM_LIMIT_BYTES):
    return pltpu.CompilerParams(dimension_semantics=semantics, vmem_limit_bytes=vmem)


def _bdot(a, b):
    return jnp.dot(a.astype(BF16), b.astype(BF16), preferred_element_type=F32)


def _bdot_nt(a, b):
    return lax.dot_general(a.astype(BF16), b.astype(BF16), (((1,), (1,)), ((), ())),
                           preferred_element_type=F32)


def _split2(x):
    hi = x.astype(BF16)
    lo = (x - hi.astype(F32)).astype(BF16)
    return hi, lo


def _split3(x):
    hi = x.astype(BF16)
    r1 = x - hi.astype(F32)
    mid = r1.astype(BF16)
    lo = (r1 - mid.astype(F32)).astype(BF16)
    return hi, mid, lo


def _dot3(a, b):
    ah, al = _split2(a)
    bh, bl = _split2(b)
    d = lambda u, v: jnp.dot(u, v, preferred_element_type=F32)
    return d(ah, bh) + (d(ah, bl) + d(al, bh))


def _dot_exact_left(m, x):
    hi, mid, lo = _split3(x)
    d = lambda v: jnp.dot(m, v, preferred_element_type=F32)
    return d(hi) + (d(mid) + d(lo))


def _unit_tri_inverse(a, row, col, mm):
    eye = (row == col).astype(F32)
    a8 = jnp.where((row >> 3) == (col >> 3), a, 0.0)
    a2 = mm(a8, a8)
    p = eye + a8
    p = p + mm(p, a2)
    a4 = mm(a2, a2)
    x = p + mm(p, a4)
    for sh in (3, 4, 5):
        m = ((row >> (sh + 1)) == (col >> (sh + 1))) & ((row >> sh) != (col >> sh))
        x = x + mm(mm(x, jnp.where(m, a, 0.0)), x)
    return x


def _delta_scan_kernel(r_ref, kb_ref, ka_ref, kk_ref, v_ref, g_ref, y_ref, t_ref, *,
                       head_dim, reverse, scalar_decay, beta_excl):
    TT, W, C = ROW_TILE, V7X_MXU_DIM, SCAN_CHUNK
    n_heads = W // head_dim
    hshift = int(math.log2(head_dim))

    @pl.when(pl.program_id(2) == 0)
    def _():
        t_ref[...] = jnp.zeros_like(t_ref)

    row = lax.broadcasted_iota(jnp.int32, (TT, TT), 0)
    col = lax.broadcasted_iota(jnp.int32, (TT, TT), 1)
    same = (row >> 6) == (col >> 6)
    if reverse:
        strict, incl = same & (col > row), same & (col >= row)
    else:
        strict, incl = same & (col < row), same & (col <= row)

    g = g_ref[0]
    gi = _dot_exact_left(incl.astype(BF16), g)
    gc = _dot_exact_left(same.astype(BF16), g)
    gb = gi - g if beta_excl else gi
    r, kb, ka, kk, v = r_ref[0], kb_ref[0], ka_ref[0], kk_ref[0], v_ref[0]
    e_rest = jnp.exp(gc - gi)
    r_s = r * jnp.exp(gi)
    kb_s = kb * jnp.exp(gb)
    ka_e = ka * e_rest
    kk_e = kk * e_rest
    if scalar_decay:
        assert head_dim == V7X_LANES
        gram_r, gram_b, gram_a, gram_k = r, kb, ka, kk
    else:
        inv = jnp.exp(-gi)
        gram_r, gram_b, gram_a, gram_k = r_s, kb_s, ka * inv, kk * inv

    lane_head = lax.broadcasted_iota(jnp.int32, (TT, W), 1) >> hshift
    b_hat = jnp.zeros((TT, W), F32)
    w1 = jnp.zeros((TT, W), F32)
    r_hat = r_s
    y0 = jnp.zeros((TT, W), F32)
    for j in range(n_heads):
        mj = lane_head == j
        bl = jnp.where(mj, gram_b, 0.0)
        rl = jnp.where(mj, gram_r, 0.0)
        a_ba, a_bk = _bdot_nt(bl, gram_a), _bdot_nt(bl, gram_k)
        a_ra, a_rk = _bdot_nt(rl, gram_a), _bdot_nt(rl, gram_k)
        if scalar_decay:
            sl = slice(j * head_dim, (j + 1) * head_dim)
            gi_m = jnp.concatenate([gi[:, sl]] * n_heads, axis=1)
            gb_m = jnp.concatenate([gb[:, sl]] * n_heads, axis=1)
            gi_t = gi_m.T
            d_b = jnp.exp(jnp.minimum(gb_m - gi_t, 0.0))
            d_r = jnp.exp(jnp.minimum(gi_m - gi_t, 0.0))
            a_ba, a_bk, a_ra, a_rk = a_ba * d_b, a_bk * d_b, a_ra * d_r, a_rk * d_r
        a_ba = jnp.where(strict, a_ba, 0.0)
        a_bk = jnp.where(strict, a_bk, 0.0)
        a_ra = jnp.where(incl, a_ra, 0.0)
        a_rk = jnp.where(incl, a_rk, 0.0)
        x = _unit_tri_inverse(a_ba, row, col, _dot3)
        bj = _bdot(x, kb_s)
        wj = _bdot(x, _bdot(a_bk, v))
        b_hat = jnp.where(mj, bj, b_hat)
        w1 = jnp.where(mj, wj, w1)
        r_hat = r_hat + jnp.where(mj, _bdot(a_ra, bj), 0.0)
        y0 = jnp.where(mj, _bdot(a_ra, wj) + _bdot(a_rk, v), y0)

    rw = lax.broadcasted_iota(jnp.int32, (W, W), 0)
    cw = lax.broadcasted_iota(jnp.int32, (W, W), 1)
    blockdiag = (rw >> hshift) == (cw >> hshift)
    eye_w = rw == cw
    t = t_ref[...]
    for c in (range(TT // C - 1, -1, -1) if reverse else range(TT // C)):
        sl = slice(c * C, (c + 1) * C)
        y_ref[0, sl, :] = _bdot(r_hat[sl], t) + y0[sl]
        ka_t = ka_e[sl].T
        kk_t = kk_e[sl].T
        decay_row = jnp.exp(gc[c * C:c * C + 1, :])
        decay_col = jnp.sum(jnp.where(eye_w, decay_row, 0.0), axis=1, keepdims=True)
        mix = jnp.where(blockdiag, _bdot(ka_t, b_hat[sl]), 0.0)
        add = jnp.where(blockdiag, _bdot(ka_t, w1[sl]) + _bdot(kk_t, v[sl]), 0.0)
        t = decay_col * t + _bdot(mix, t) + add
    t_ref[...] = t


def delta_scan(r, kb, ka, kk, v, g, *, head_dim, reverse, scalar_decay, beta_excl):
    B, T, HW = r.shape
    n_t = T // ROW_TILE
    n_g = HW // V7X_MXU_DIM
    if reverse:
        tile = lambda i: jnp.where(i == 0, 0, n_t - i)
    else:
        tile = lambda i: i
    spec = pl.BlockSpec((1, ROW_TILE, V7X_MXU_DIM), lambda b, h, i: (b, tile(i), h))
    kern = functools.partial(_delta_scan_kernel, head_dim=head_dim, reverse=reverse,
                             scalar_decay=scalar_decay, beta_excl=beta_excl)
    return pl.pallas_call(
        kern,
        grid=(B, n_g, n_t),
        in_specs=[spec] * 6,
        out_specs=spec,
        out_shape=jax.ShapeDtypeStruct((B, T, HW), F32),
        scratch_shapes=[pltpu.VMEM((V7X_MXU_DIM, V7X_MXU_DIM), F32)],
        compiler_params=_params(("parallel", "parallel", "arbitrary")),
        name="delta_scan_rev" if reverse else "delta_scan_fwd",
    )(r, kb, ka, kk, v, g)


def _matmul_kernel(*refs, nk, passes, has_bias, a_silu):
    a_ref, b_ref = refs[0], refs[1]
    bias_ref = refs[2] if has_bias else None
    o_ref = refs[2 + has_bias]
    acc_ref = refs[3 + has_bias] if nk > 1 else None
    a = a_ref[...]
    if a_silu:
        a = a.astype(F32)
        a = a * jax.nn.sigmoid(a)
    if passes == 3:
        part = _dot3(a.astype(F32), b_ref[...].astype(F32))
    else:
        part = _bdot(a, b_ref[...])

    def finish(acc):
        if has_bias:
            acc = acc + bias_ref[...]
        o_ref[...] = acc.astype(o_ref.dtype)

    if nk == 1:
        finish(part)
    else:
        k = pl.program_id(2)

        @pl.when(k == 0)
        def _():
            acc_ref[...] = part

        @pl.when(k > 0)
        def _():
            acc_ref[...] += part

        @pl.when(k == nk - 1)
        def _():
            finish(acc_ref[...])


def matmul(a, b, *, tm, tn, tk=None, bias=None, out_dtype=F32, passes=1, a_silu=False, layer=None,
           name="matmul"):
    M, K = a.shape
    N = b.shape[-1]
    tk = K if tk is None else tk
    assert M % tm == 0 and N % tn == 0 and K % tk == 0, (a.shape, b.shape, tm, tn, tk)
    nk = K // tk
    if layer is None:
        b_spec = pl.BlockSpec((tk, tn), lambda i, j, k: (k, j))
    else:
        b_spec = pl.BlockSpec((pl.Squeezed(), tk, tn), lambda i, j, k: (layer, k, j))
    in_specs = [pl.BlockSpec((tm, tk), lambda i, j, k: (i, k)), b_spec]
    args = [a, b]
    if bias is not None:
        in_specs.append(pl.BlockSpec((1, tn), lambda i, j, k: (0, j)))
        args.append(bias.reshape(1, N).astype(F32))
    kern = functools.partial(_matmul_kernel, nk=nk, passes=passes, has_bias=bias is not None, a_silu=a_silu)
    return pl.pallas_call(
        kern,
        grid=(M // tm, N // tn, nk),
        in_specs=in_specs,
        out_specs=pl.BlockSpec((tm, tn), lambda i, j, k: (i, j)),
        out_shape=jax.ShapeDtypeStruct((M, N), out_dtype),
        scratch_shapes=[pltpu.VMEM((tm, tn), F32)] if nk > 1 else [],
        compiler_params=_params(("parallel", "parallel", "arbitrary")),
        name=name,
    )(*args)


def _norm_mod_kernel(x_ref, g_ref, mod_ref, o_ref, *, which):
    x = x_ref[0]
    y = x * lax.rsqrt(jnp.mean(x * x, axis=-1, keepdims=True) + NORM_EPS) * g_ref[...]
    shift = mod_ref[0, 3 * which:3 * which + 1, :]
    scale = mod_ref[0, 3 * which + 1:3 * which + 2, :]
    o_ref[0] = (y * (1.0 + scale) + shift).astype(o_ref.dtype)


def norm_mod(x, g, mod, *, which, ctx_tiles, out_dtype=BF16):
    B, T, D = x.shape
    ctx_row = mod.shape[0] - 1
    sel = lambda b, j: jnp.where(j < ctx_tiles, ctx_row, b)
    return pl.pallas_call(
        functools.partial(_norm_mod_kernel, which=which),
        grid=(B, T // ROW_TILE),
        in_specs=[pl.BlockSpec((1, ROW_TILE, D), lambda b, j: (b, j, 0)),
                  pl.BlockSpec((1, D), lambda b, j: (0, 0)),
                  pl.BlockSpec((1, 6, D), lambda b, j: (sel(b, j), 0, 0))],
        out_specs=pl.BlockSpec((1, ROW_TILE, D), lambda b, j: (b, j, 0)),
        out_shape=jax.ShapeDtypeStruct((B, T, D), out_dtype),
        compiler_params=_params(("parallel", "parallel")),
        name="norm_mod",
    )(x, g.reshape(1, D), mod)


def _na_kernel(rpb_ref, q_ref, k_ref, v_ref, kc_ref, vc_ref, o_ref, tb_ref):
    h = pl.program_id(1)
    GW, HD = GRID_W, V7X_LANES
    n_rows = q_ref.shape[1] // GW
    wr = NA_WIN_R
    n_dc = 2 * NA_WIN_C - 1
    qcol = lax.broadcasted_iota(jnp.int32, (GW, 2 * GW), 0)
    lane = lax.broadcasted_iota(jnp.int32, (GW, 2 * GW), 1)
    kcol = lane & (GW - 1)
    upper = lane >= GW
    dci = jnp.clip(kcol - qcol + NA_WIN_C - 1, 0, n_dc - 1)
    cstart = jnp.clip(qcol - NA_WIN_C // 2, 0, GW - NA_WIN_C)
    col_ok = (kcol >= cstart) & (kcol < cstart + NA_WIN_C)
    for dr in range(2 * wr - 2):
        acc = jnp.zeros((GW, 2 * GW), F32)
        for j in range(n_dc):
            lo = rpb_ref[h, dr * n_dc + j]
            hi = rpb_ref[h, (dr + 1) * n_dc + j]
            acc = jnp.where(dci == j, jnp.where(upper, hi, lo), acc)
        tb_ref[dr] = acc
    ok4 = jnp.concatenate([col_ok] * (wr // 2), axis=1)
    kc = kc_ref[0]
    vc = vc_ref[0]
    scale = HD ** -0.5

    def body(r, carry):
        rs = jnp.clip(r - wr // 2, 0, n_rows - wr)
        q = q_ref[0, pl.ds(pl.multiple_of(r * GW, GW), GW), :] * scale
        k0 = pl.multiple_of(rs * GW, GW)
        kw = k_ref[0, pl.ds(k0, wr * GW), :]
        vw = v_ref[0, pl.ds(k0, wr * GW), :]
        dr0 = rs - r + wr - 1
        bias = jnp.concatenate([tb_ref[dr0 + 2 * w] for w in range(wr // 2)], axis=1)
        s_lat = jnp.where(ok4, _bdot_nt(q, kw) + bias, NEG_INF)
        s_ctx = _bdot_nt(q, kc)
        m = jnp.maximum(jnp.max(s_lat, axis=-1, keepdims=True), jnp.max(s_ctx, axis=-1, keepdims=True))
        p_lat = jnp.exp(s_lat - m)
        p_ctx = jnp.exp(s_ctx - m)
        den = jnp.sum(p_lat, axis=-1, keepdims=True) + jnp.sum(p_ctx, axis=-1, keepdims=True)
        o = (_bdot(p_lat, vw) + _bdot(p_ctx, vc)) / den
        o_ref[0, pl.ds(pl.multiple_of(r * GW, GW), GW), :] = o
        return carry

    lax.fori_loop(0, n_rows, body, 0)


def neighbourhood_attention(pl_lat, kv_ctx, rpb):
    B, S, _ = pl_lat.shape
    Lc = kv_ctx.shape[1]
    H, HD = NA_HEADS, V7X_LANES
    lat = lambda off: pl.BlockSpec((1, S, HD), lambda b, h: (b, 0, off + h))
    cx = lambda off: pl.BlockSpec((1, Lc, HD), lambda b, h: (b, 0, off + h))
    return pl.pallas_call(
        _na_kernel,
        grid=(B, H),
        in_specs=[pl.BlockSpec(memory_space=pltpu.SMEM), lat(0), lat(H), lat(2 * H), cx(0), cx(H)],
        out_specs=pl.BlockSpec((1, S, HD), lambda b, h: (b, 0, h)),
        out_shape=jax.ShapeDtypeStruct((B, S, H * HD), F32),
        scratch_shapes=[pltpu.VMEM((2 * NA_WIN_R - 2, GRID_W, 2 * GRID_W), F32)],
        compiler_params=_params(("parallel", "parallel")),
        name="neighbourhood_attention",
    )(rpb.reshape(H, -1), pl_lat, pl_lat, pl_lat, kv_ctx, kv_ctx)


DFT_N1 = 64
DFT_N2 = 128
DFT_SUB = 8


@functools.lru_cache(maxsize=None)
def _dft_constants():
    n1, n2 = DFT_N1, DFT_N2
    n = n1 * n2
    a = np.arange(n1)
    f1 = np.exp(-2j * np.pi * np.outer(a, a) / n1)
    half = f1[:, :n1 // 2]
    eye = np.eye(DFT_SUB)
    fwd = np.kron(half, eye)
    inv = np.kron(np.conj(f1)[:n1 // 2, :], eye) / n
    b = np.arange(n2)
    f2 = np.exp(-2j * np.pi * np.outer(b, b) / n2)
    tw = np.exp(-2j * np.pi * np.outer(a, b) / n)
    m = f2[None, :, :] * tw[:, None, :]
    mt = np.transpose(m, (0, 2, 1))
    c = lambda z: (np.ascontiguousarray(z.real, dtype=np.float32), np.ascontiguousarray(z.imag, dtype=np.float32))
    return c(fwd) + c(inv) + c(m) + c(mt)


def _hyena_conv_kernel(hf_ref, hb_ref, z_ref, fwr_ref, fwi_ref, ivr_ref, ivi_ref,
                       mr_ref, mi_ref, mtr_ref, mti_ref, o_ref, ar_ref, ai_ref, kr_ref, ki_ref):
    L, TC = hf_ref.shape
    N1, N2, SUB = DFT_N1, DFT_N2, DFT_SUB
    nblk = N2 // SUB

    def stage1(load):
        def blk(nb, carry):
            off = pl.multiple_of(nb * SUB, SUB)
            xs = jnp.concatenate([load(n1 * N2 + off) for n1 in range(N1 // 2)], axis=0)
            yr = _bdot(fwr_ref[...], xs)
            yi = _bdot(fwi_ref[...], xs)
            for k1 in range(N1):
                ar_ref[pl.ds(k1 * N2 + off, SUB), :] = yr[k1 * SUB:(k1 + 1) * SUB]
                ai_ref[pl.ds(k1 * N2 + off, SUB), :] = yi[k1 * SUB:(k1 + 1) * SUB]
            return carry

        lax.fori_loop(0, nblk, blk, 0)

    def spectrum(k1):
        rows = pl.ds(pl.multiple_of(k1 * N2, N2), N2)
        a_r, a_i = ar_ref[rows, :], ai_ref[rows, :]
        m_r, m_i = mr_ref[k1], mi_ref[k1]
        return rows, _bdot(m_r, a_r) - _bdot(m_i, a_i), _bdot(m_r, a_i) + _bdot(m_i, a_r)

    stage1(lambda r0: hf_ref[pl.ds(r0, SUB), :])

    def filt_f(k1, carry):
        rows, x_r, x_i = spectrum(k1)
        kr_ref[rows, :] = x_r
        ki_ref[rows, :] = x_i
        return carry

    lax.fori_loop(0, N1, filt_f, 0)
    sub_iota = lax.broadcasted_iota(jnp.int32, (SUB, TC), 0)
    stage1(lambda r0: jnp.where(sub_iota + r0 == 0, 0.0, hb_ref[pl.ds(r0, SUB), :]))

    def filt_b(k1, carry):
        rows, x_r, x_i = spectrum(k1)
        kr_ref[rows, :] += x_r
        ki_ref[rows, :] -= x_i
        return carry

    lax.fori_loop(0, N1, filt_b, 0)

    for b in range(z_ref.shape[0]):
        stage1(lambda r0: z_ref[b, pl.ds(r0, SUB), :])

        def conv(k1, carry):
            rows, x_r, x_i = spectrum(k1)
            f_r, f_i = kr_ref[rows, :], ki_ref[rows, :]
            y_r = x_r * f_r - x_i * f_i
            y_i = x_r * f_i + x_i * f_r
            t_r, t_i = mtr_ref[k1], mti_ref[k1]
            ar_ref[rows, :] = _bdot(t_r, y_r) + _bdot(t_i, y_i)
            ai_ref[rows, :] = _bdot(t_r, y_i) - _bdot(t_i, y_r)
            return carry

        lax.fori_loop(0, N1, conv, 0)
        def last(nb, carry):
            off = pl.multiple_of(nb * SUB, SUB)
            br = jnp.concatenate([ar_ref[pl.ds(k1 * N2 + off, SUB), :] for k1 in range(N1)], axis=0)
            bi = jnp.concatenate([ai_ref[pl.ds(k1 * N2 + off, SUB), :] for k1 in range(N1)], axis=0)
            y = _bdot(ivr_ref[...], br) - _bdot(ivi_ref[...], bi)
            for n1 in range(N1 // 2):
                o_ref[b, pl.ds(n1 * N2 + off, SUB), :] = y[n1 * SUB:(n1 + 1) * SUB]
            return carry

        lax.fori_loop(0, nblk, last, 0)


def hyena_long_conv(z, h_f, h_b, *, tc=V7X_LANES):
    B, L, C = z.shape
    assert L == DFT_N1 * DFT_N2 // 2
    consts = [jnp.asarray(a, BF16) for a in _dft_constants()]
    whole = lambda a: pl.BlockSpec(a.shape, lambda j: (0,) * a.ndim)
    col = pl.BlockSpec((L, tc), lambda j: (0, j))
    bcol = pl.BlockSpec((B, L, tc), lambda j: (0, 0, j))
    return pl.pallas_call(
        _hyena_conv_kernel,
        grid=(C // tc,),
        in_specs=[col, col, bcol] + [whole(a) for a in consts],
        out_specs=bcol,
        out_shape=jax.ShapeDtypeStruct((B, L, C), F32),
        scratch_shapes=[pltpu.VMEM((2 * L, tc), F32)] * 4,
        compiler_params=_params(("parallel",)),
        name="hyena_long_conv",
    )(h_f, h_b, z, *consts)


def _moe_up_kernel(x_ref, wg_ref, wu_ref, gate_ref, o_ref):
    x = x_ref[...]
    hg = _bdot(x, wg_ref[...])
    hu = _bdot(x, wu_ref[...])
    o_ref[...] = (hg * jax.nn.sigmoid(hg) * hu * gate_ref[0]).astype(o_ref.dtype)


def moe_up(t, w_gate, w_up, gates_t, *, layer, tm, tf=256):
    n, D = t.shape
    _, E, _, F = w_gate.shape
    nf = F // tf
    wspec = pl.BlockSpec((pl.Squeezed(), pl.Squeezed(), D, tf), lambda i, e, j: (layer, e, 0, j))
    return pl.pallas_call(
        _moe_up_kernel,
        grid=(n // tm, E, nf),
        in_specs=[pl.BlockSpec((tm, D), lambda i, e, j: (i, 0)), wspec, wspec,
                  pl.BlockSpec((1, tm, 1), lambda i, e, j: (e, i, 0))],
        out_specs=pl.BlockSpec((tm, tf), lambda i, e, j: (i, e * nf + j)),
        out_shape=jax.ShapeDtypeStruct((n, E * F), BF16),
        compiler_params=_params(("parallel", "arbitrary", "arbitrary")),
        name="moe_up",
    )(t, w_gate, w_up, gates_t)


def _pick(n, options):
    for t in options:
        if n % t == 0:
            return t
    raise ValueError(f"no tile for {n} in {options}")


def _mm(a, w, **kw):
    tm = _pick(a.shape[0], (1088, 1024, 768, 512, 256, 8))
    tn = _pick(w.shape[1], (512, 640, 384, 256, 128))
    return matmul(a, w, tm=tm, tn=tn, **kw)


def _pad_to(a, axis, mult):
    pad = (-a.shape[axis]) % mult
    if pad == 0:
        return a
    widths = [(0, 0)] * a.ndim
    widths[axis] = (0, pad)
    return jnp.pad(a, widths)


def _silu(x):
    return x * jax.nn.sigmoid(x)


def _seq_parts(x, lc, fn):
    if lc == 0:
        return fn(x)
    return jnp.concatenate([fn(x[:, :lc]), fn(x[:, lc:])], axis=1)


def _dwconv(x, w):
    k, length = w.shape[0], x.shape[1]
    xp = jnp.pad(x, ((0, 0), (k // 2, k // 2), (0, 0)))
    return sum(xp[:, j:j + length] * w[j] for j in range(k))


def _centred_shift(p):
    prev = jnp.pad(p, ((0, 0), (1, 0), (0, 0)))[:, :-1]
    nxt = jnp.pad(p, ((0, 0), (0, 1), (0, 0)))[:, 1:]
    return 0.5 * (prev + nxt)


def _l2norm(t):
    return t * lax.rsqrt(jnp.sum(t * t, -1, keepdims=True) + 1e-6)


def _axial_rope(t):
    length, d = t.shape[1], t.shape[-1]
    nf = d // 4
    pos = jnp.arange(length)
    inv = ROPE_BASE ** (-jnp.arange(nf, dtype=F32) / nf)

    def cs(p):
        ang = p.astype(F32)[:, None] * inv[None, :]
        return jnp.cos(ang), jnp.sin(ang)

    cr, sr = cs(pos // GRID_W)
    cc, sc = cs(pos % GRID_W)
    cos = jnp.concatenate([cr, cr, cc, cc], -1)[None, :, None, :]
    sin = jnp.concatenate([-sr, sr, -sc, sc], -1)[None, :, None, :]
    swapped = jnp.flip(t.reshape(*t.shape[:-1], 2, 2, nf), axis=-2).reshape(t.shape)
    return t * cos + swapped * sin


def _mixer_ab(h, lc, w_in, dn_conv_w, dn_a_log, dn_dt_bias, dn_norm_w,
              rw_mu, rw_w0, rw_w2, rw_a0, rw_a2, rw_g2, rw_k_k, rw_k_a, rw_r_k, rw_ln_w, rw_ln_b, w_out):
    B, T, D = h.shape
    mw = D // 2
    hd_dn = mw // DN_HEADS
    h_rw = mw // RW_HD
    p_a = 4 * mw + 4 * DN_HEADS
    h2 = h.reshape(B * T, D)
    pa = _mm(h2, _pad_to(w_in[:, :p_a], 1, V7X_LANES).astype(BF16)).reshape(B, T, -1)
    pb = _mm(h2, _pad_to(w_in[:, p_a:], 1, V7X_LANES).astype(BF16)).reshape(B, T, -1)

    qkv = _seq_parts(pa[..., :3 * mw], lc, lambda u: _silu(_dwconv(u, dn_conv_w)))
    q, k, v = (t.reshape(B, T, DN_HEADS, hd_dn) for t in jnp.split(qkv, 3, axis=-1))
    rope_lat = lambda t: jnp.concatenate([t[:, :lc], _axial_rope(t[:, lc:])], axis=1)
    q, k = rope_lat(_l2norm(q)), rope_lat(_l2norm(k))
    q = q * hd_dn ** -0.5
    z = pa[..., 3 * mw:4 * mw]
    bg = pa[..., 4 * mw:p_a].reshape(B, T, 4, DN_HEADS)
    beta = jax.nn.sigmoid(bg[:, :, :2])
    g = -jnp.exp(dn_a_log) * jax.nn.softplus(bg[:, :, 2:] + dn_dt_bias)
    flat = lambda t: t.reshape(B, T, mw)
    o = 0.0
    for d in range(2):
        bk = k * beta[:, :, d, :, None]
        gd = jnp.broadcast_to(g[:, :, d, :, None], k.shape)
        o = o + delta_scan(flat(q), flat(k), flat(-bk), flat(bk), flat(v), flat(gd),
                           head_dim=hd_dn, reverse=d == 1, scalar_decay=True, beta_excl=False)
    o = o.reshape(B, T, DN_HEADS, hd_dn)
    o = o * lax.rsqrt(jnp.mean(o * o, -1, keepdims=True) + NORM_EPS) * dn_norm_w
    dn = flat(o * _silu(z).reshape(o.shape))

    p_b = w_in.shape[1] - p_a
    p = pb[..., :p_b]
    p = p + (_seq_parts(p, lc, _centred_shift) - p) * rw_mu
    r, k, v = p[..., :mw], p[..., mw:2 * mw], p[..., 2 * mw:3 * mw]
    o1 = 3 * mw
    n_w, n_a = rw_w2.shape[1], rw_a2.shape[1]
    o2 = o1 + 2 * n_w
    o3 = o2 + 2 * n_a
    wl = p[..., o1:o2].reshape(B * T, 2, n_w)
    al = p[..., o2:o3].reshape(B * T, 2, n_a)
    gl = p[..., o3:].reshape(B * T, -1)
    gate = _mm(_pad_to(jax.nn.sigmoid(gl), 1, V7X_LANES), _pad_to(rw_g2, 0, V7X_LANES)).reshape(B, T, mw)
    heads = lambda t: t.reshape(B, T, h_rw, RW_HD)
    kk = flat(_l2norm(heads(k * rw_k_k)))
    y = 0.0
    for d in range(2):
        w = -jax.nn.softplus(-(rw_w0[d] + _mm(jnp.tanh(wl[:, d]), rw_w2[d]).reshape(B, T, mw))) - 0.5
        a = jax.nn.sigmoid(rw_a0[d] + _mm(al[:, d], rw_a2[d]).reshape(B, T, mw))
        kd = k * (1.0 + (a - 1.0) * rw_k_a)
        y = y + delta_scan(r, kk, -(kk * a), kd, v, -jnp.exp(w),
                           head_dim=RW_HD, reverse=d == 1, scalar_decay=False, beta_excl=True)
    y = heads(y)
    mu = jnp.mean(y, -1, keepdims=True)
    var = jnp.mean(jnp.square(y - mu), -1, keepdims=True)
    yn = flat((y - mu) * lax.rsqrt(var + RW_LN_EPS)) * rw_ln_w + rw_ln_b
    bonus = flat(jnp.sum(heads(r) * heads(k) * rw_r_k, -1, keepdims=True) * heads(v))
    rw = (yn + bonus) * gate

    cat = jnp.concatenate([dn, rw], -1).astype(BF16).reshape(B * T, D)
    return _mm(cat, w_out.astype(BF16)).reshape(B, T, D)


def _hyena_filters(length, ch, w1, b1, freq, w2, b2, w3, b3, w4):
    t = jnp.linspace(0.0, 1.0, length, dtype=F32)
    bands = (HY_EMB - 1) // 2
    wpos = 2 * math.pi * jnp.arange(length, dtype=F32) / length
    fb = jnp.linspace(1e-4, bands - 1, bands, dtype=F32)
    ang = wpos[:, None] * fb[None, :]
    z = jnp.concatenate([t[:, None], jnp.cos(ang), -jnp.sin(ang)], axis=-1)
    dense = lambda u, w: _mm(_pad_to(u, 1, V7X_LANES), _pad_to(_pad_to(w, 0, V7X_LANES), 1, V7X_LANES),
                             passes=3)[:, :w.shape[1]]
    h = jnp.sin(freq * (dense(z, w1) + b1))
    h = jnp.sin(freq * (dense(h, w2) + b2))
    h = jnp.sin(freq * (dense(h, w3) + b3))
    h = dense(h, w4).reshape(length, 2, ch)
    deltas = jnp.abs(jnp.linspace(HY_MIN_DECAY, HY_MAX_DECAY, ch, dtype=F32))
    window = jnp.exp(-t[:, None] * deltas[None, :])
    return h[:, 0] * window, h[:, 1] * window


def _mixer_cd(hc, hl, w_in, na_rpb, hy_conv_w, hy_conv_b, filt, hy_skip, w_out):
    B, S, D = hl.shape
    lc = hc.shape[1]
    mw = D // 2
    w_bf = w_in.astype(BF16)
    pl_ = _mm(hl.reshape(B * S, D), w_bf).reshape(B, S, -1)
    kv_c = _mm(hc.reshape(B * lc, D), w_bf[:, mw:3 * mw]).reshape(B, lc, 2 * mw)
    att = neighbourhood_attention(pl_, kv_c, na_rpb)
    u = _dwconv(pl_[..., 3 * mw:], hy_conv_w) + hy_conv_b
    x0, x1, v = jnp.split(u, 3, axis=-1)
    h_f, h_b = _hyena_filters(S, mw, *filt)
    z = x1 * v
    hy = x0 * (hyena_long_conv(z, h_f, h_b) + z * hy_skip)
    cat = jnp.concatenate([att, hy], -1).astype(BF16).reshape(B * S, D)
    return _mm(cat, w_out.astype(BF16)).reshape(B, S, D)


def _moe(t_bf, t_f32, router_w, router_bias, w_gate, w_up, w_down, layer):
    n, D = t_bf.shape
    _, E, _, F = w_gate.shape
    logits = _mm(t_f32, _pad_to(router_w, 1, V7X_LANES), passes=3)[:, :E]
    s = jax.nn.sigmoid(logits)
    sel = s + router_bias
    per = E // N_GROUPS
    grp = lax.top_k(sel.reshape(n, N_GROUPS, per), 2)[0].sum(-1)
    gidx = lax.top_k(grp, 1)[1]
    gmask = jax.nn.one_hot(gidx, N_GROUPS, dtype=F32).sum(1) > 0
    emask = jnp.repeat(gmask, per, axis=1)
    eidx = lax.top_k(jnp.where(emask, sel, NEG_INF), 2)[1]
    wsel = jnp.take_along_axis(s, eidx, axis=1)
    wsel = wsel / jnp.sum(wsel, -1, keepdims=True)
    gates = jnp.sum(jax.nn.one_hot(eidx, E, dtype=F32) * wsel[..., None], axis=1)
    tm = _pick(n, (1088, 1024, 768, 512, 256))
    act = moe_up(t_bf, w_gate, w_up, gates.T[:, :, None], layer=layer, tm=tm)
    return matmul(act, w_down.reshape(-1, E * F, D), tm=tm, tn=1024, tk=1024, layer=layer, name="moe_down")


def kernel(x, c, ctx, c_ctx, ada_w, ada_b, norm1_g, norm2_g, final_g, ab_w_in, dn_conv_w, dn_a_log, dn_dt_bias, dn_norm_w, rw_mu, rw_w0, rw_w2, rw_a0, rw_a2, rw_g2, rw_k_k, rw_k_a, rw_r_k, rw_ln_w, rw_ln_b, ab_w_out, cd_w_in, na_rpb, hy_conv_w, hy_conv_b, hy_w1, hy_b1, hy_freq, hy_w2, hy_b2, hy_w3, hy_b3, hy_w4, hy_skip, cd_w_out, router_w, router_bias, moe_w_gate, moe_w_up, moe_w_down):
    B, S, D = x.shape
    lc = ctx.shape[1]
    assert ada_w.shape[0] == 2 and lc == ROW_TILE and S % ROW_TILE == 0 and B + 1 <= V7X_SUBLANES
    T = lc + S
    ctx_tiles = lc // ROW_TILE
    cond = jnp.concatenate([c, c_ctx[None], jnp.zeros((V7X_SUBLANES - B - 1, D), F32)], axis=0)
    mods = [matmul(cond, ada_w, tm=V7X_SUBLANES, tn=512, bias=ada_b[l], a_silu=True, layer=l,
                   name="adaln")[:B + 1].reshape(B + 1, 6, D) for l in range(2)]

    def rows(m, idx):
        return jnp.concatenate([jnp.broadcast_to(m[B, idx], (B, lc, D)),
                                jnp.broadcast_to(m[:B, idx][:, None, :], (B, S, D))], axis=1)

    m = mods[0]
    xa = jnp.concatenate([ctx, x], axis=1)
    h = norm_mod(xa, norm1_g[0], m, which=0, ctx_tiles=ctx_tiles)
    mix = _mixer_ab(h, lc, ab_w_in[0], dn_conv_w[0], dn_a_log[0], dn_dt_bias[0], dn_norm_w[0],
                    rw_mu[0], rw_w0[0], rw_w2[0], rw_a0[0], rw_a2[0], rw_g2[0], rw_k_k[0], rw_k_a[0],
                    rw_r_k[0], rw_ln_w[0], rw_ln_b[0], ab_w_out[0])
    xa = xa + rows(m, 2) * mix
    h_bf = norm_mod(xa, norm2_g[0], m, which=1, ctx_tiles=ctx_tiles)
    h_f32 = norm_mod(xa, norm2_g[0], m, which=1, ctx_tiles=ctx_tiles, out_dtype=F32)
    f = _moe(h_bf.reshape(B * T, D), h_f32.reshape(B * T, D), router_w, router_bias,
             moe_w_gate, moe_w_up, moe_w_down, 0).reshape(B, T, D)
    xa = xa + rows(m, 5) * f

    m = mods[1]
    cx, xl = xa[:, :lc], xa[:, lc:]
    hl = norm_mod(xl, norm1_g[1], m, which=0, ctx_tiles=0)
    hc = norm_mod(cx, norm1_g[1], m, which=0, ctx_tiles=ctx_tiles)
    filt = (hy_w1[0], hy_b1[0], hy_freq[0], hy_w2[0], hy_b2[0], hy_w3[0], hy_b3[0], hy_w4[0])
    ml = _mixer_cd(hc, hl, cd_w_in[0], na_rpb[0], hy_conv_w[0], hy_conv_b[0], filt, hy_skip[0], cd_w_out[0])
    xl = xl + m[:B, 2][:, None, :] * ml
    h_bf = norm_mod(xl, norm2_g[1], m, which=1, ctx_tiles=0)
    h_f32 = norm_mod(xl, norm2_g[1], m, which=1, ctx_tiles=0, out_dtype=F32)
    f = _moe(h_bf.reshape(B * S, D), h_f32.reshape(B * S, D), router_w, router_bias,
             moe_w_gate, moe_w_up, moe_w_down, 1).reshape(B, S, D)
    xl = xl + m[:B, 5][:, None, :] * f
    return xl * lax.rsqrt(jnp.mean(xl * xl, -1, keepdims=True) + NORM_EPS) * final_g
```

```python
import functools
import math

import jax
import jax.numpy as jnp
import numpy as np
from jax import lax
from jax.experimental import pallas as pl
from jax.experimental.pallas import tpu as pltpu

F32 = jnp.float32
BF16 = jnp.bfloat16

V7X_LANES = 128
V7X_SUBLANES = 8
V7X_MXU_DIM = 256
V7X_VMEM_LIMIT_BYTES = 56 * 1024 * 1024

GRID_W = 64
NORM_EPS = 1e-6
NEG_INF = -1e30
ROPE_BASE = 10000.0
DN_HEADS = 16
DN_CONV = 5
RW_HD = 64
RW_LN_EPS = 64e-5
NA_HEADS = 16
NA_WIN_R = 8
NA_WIN_C = 16
HY_EMB = 33
HY_TARGET = 1e-2
HY_MAX_DECAY = math.log(HY_TARGET) / 0.3
HY_MIN_DECAY = math.log(HY_TARGET) / 1.5
N_EXPERTS = 16
N_GROUPS = 4

ROW_TILE = 256
SCAN_CHUNK = 64


def _params(semantics, vmem=V7X_VMEM_LIMIT_BYTES):
    return pltpu.CompilerParams(dimension_semantics=semantics, vmem_limit_bytes=vmem)


def _bdot(a, b):
    return jnp.dot(a.astype(BF16), b.astype(BF16), preferred_element_type=F32)


def _bdot_nt(a, b):
    return lax.dot_general(a.astype(BF16), b.astype(BF16), (((1,), (1,)), ((), ())),
                           preferred_element_type=F32)


def _split2(x):
    hi = x.astype(BF16)
    lo = (x - hi.astype(F32)).astype(BF16)
    return hi, lo


def _split3(x):
    hi = x.astype(BF16)
    r1 = x - hi.astype(F32)
    mid = r1.astype(BF16)
    lo = (r1 - mid.astype(F32)).astype(BF16)
    return hi, mid, lo


def _dot3(a, b):
    ah, al = _split2(a)
    bh, bl = _split2(b)
    d = lambda u, v: jnp.dot(u, v, preferred_element_type=F32)
    return d(ah, bh) + (d(ah, bl) + d(al, bh))


def _dot_exact_left(m, x):
    hi, mid, lo = _split3(x)
    d = lambda v: jnp.dot(m, v, preferred_element_type=F32)
    return d(hi) + (d(mid) + d(lo))


def _unit_tri_inverse(a, row, col, mm):
    eye = (row == col).astype(F32)
    a8 = jnp.where((row >> 3) == (col >> 3), a, 0.0)
    a2 = mm(a8, a8)
    p = eye + a8
    p = p + mm(p, a2)
    a4 = mm(a2, a2)
    x = p + mm(p, a4)
    for sh in (3, 4, 5):
        m = ((row >> (sh + 1)) == (col >> (sh + 1))) & ((row >> sh) != (col >> sh))
        x = x + mm(mm(x, jnp.where(m, a, 0.0)), x)
    return x


def _delta_scan_kernel(*refs, head_dim, reverse, scalar_decay, beta_excl, ka_is_neg_kk):
    if ka_is_neg_kk:
        r_ref, kb_ref, kk_ref, v_ref, g_ref, y_ref, t_ref = refs
        ka_ref = None
    else:
        r_ref, kb_ref, ka_ref, kk_ref, v_ref, g_ref, y_ref, t_ref = refs
    TT, W, C = ROW_TILE, V7X_MXU_DIM, SCAN_CHUNK
    n_heads = W // head_dim
    hshift = int(math.log2(head_dim))

    @pl.when(pl.program_id(2) == 0)
    def _():
        t_ref[...] = jnp.zeros_like(t_ref)

    row = lax.broadcasted_iota(jnp.int32, (TT, TT), 0)
    col = lax.broadcasted_iota(jnp.int32, (TT, TT), 1)
    same = (row >> 6) == (col >> 6)
    if reverse:
        strict, incl = same & (col > row), same & (col >= row)
    else:
        strict, incl = same & (col < row), same & (col <= row)

    g = g_ref[0]
    gi = _dot_exact_left(incl.astype(BF16), g)
    gc = _dot_exact_left(same.astype(BF16), g)
    gb = gi - g if beta_excl else gi
    r, kb, kk, v = r_ref[0], kb_ref[0], kk_ref[0], v_ref[0]
    ka = -kk if ka_is_neg_kk else ka_ref[0]
    e_rest = jnp.exp(gc - gi)
    r_s = r * jnp.exp(gi)
    kb_s = kb * jnp.exp(gb)
    ka_e = ka * e_rest
    kk_e = kk * e_rest
    if scalar_decay:
        assert head_dim == V7X_LANES
        gram_r, gram_b, gram_a, gram_k = r, kb, ka, kk
    else:
        inv = jnp.exp(-gi)
        gram_r, gram_b, gram_a, gram_k = r_s, kb_s, ka * inv, kk * inv

    lane_head = lax.broadcasted_iota(jnp.int32, (TT, W), 1) >> hshift
    b_hat = jnp.zeros((TT, W), F32)
    w1 = jnp.zeros((TT, W), F32)
    r_hat = r_s
    y0 = jnp.zeros((TT, W), F32)
    for j in range(n_heads):
        mj = lane_head == j
        bl = jnp.where(mj, gram_b, 0.0)
        rl = jnp.where(mj, gram_r, 0.0)
        a_bk, a_rk = _bdot_nt(bl, gram_k), _bdot_nt(rl, gram_k)
        if ka_is_neg_kk:
            a_ba, a_ra = -a_bk, -a_rk
        else:
            a_ba, a_ra = _bdot_nt(bl, gram_a), _bdot_nt(rl, gram_a)
        if scalar_decay:
            sl = slice(j * head_dim, (j + 1) * head_dim)
            gi_m = jnp.concatenate([gi[:, sl]] * n_heads, axis=1)
            gb_m = jnp.concatenate([gb[:, sl]] * n_heads, axis=1)
            gi_t = gi_m.T
            d_b = jnp.exp(jnp.minimum(gb_m - gi_t, 0.0))
            d_r = jnp.exp(jnp.minimum(gi_m - gi_t, 0.0))
            a_ba, a_bk, a_ra, a_rk = a_ba * d_b, a_bk * d_b, a_ra * d_r, a_rk * d_r
        a_ba = jnp.where(strict, a_ba, 0.0)
        a_bk = jnp.where(strict, a_bk, 0.0)
        a_ra = jnp.where(incl, a_ra, 0.0)
        a_rk = jnp.where(incl, a_rk, 0.0)
        x = _unit_tri_inverse(a_ba, row, col, _bdot)
        bj = _bdot(x, kb_s)
        wj = _bdot(x, _bdot(a_bk, v))
        b_hat = jnp.where(mj, bj, b_hat)
        w1 = jnp.where(mj, wj, w1)
        r_hat = r_hat + jnp.where(mj, _bdot(a_ra, bj), 0.0)
        y0 = jnp.where(mj, _bdot(a_ra, wj) + _bdot(a_rk, v), y0)

    rw = lax.broadcasted_iota(jnp.int32, (W, W), 0)
    cw = lax.broadcasted_iota(jnp.int32, (W, W), 1)
    blockdiag = (rw >> hshift) == (cw >> hshift)
    eye_w = rw == cw
    t = t_ref[...]
    for c in (range(TT // C - 1, -1, -1) if reverse else range(TT // C)):
        sl = slice(c * C, (c + 1) * C)
        y_ref[0, sl, :] = _bdot(r_hat[sl], t) + y0[sl]
        ka_t = ka_e[sl].T
        kk_t = kk_e[sl].T
        decay_row = jnp.exp(gc[c * C:c * C + 1, :])
        decay_col = jnp.sum(jnp.where(eye_w, decay_row, 0.0), axis=1, keepdims=True)
        mix = jnp.where(blockdiag, _bdot(ka_t, b_hat[sl]), 0.0)
        add = jnp.where(blockdiag, _bdot(ka_t, w1[sl]) + _bdot(kk_t, v[sl]), 0.0)
        t = decay_col * t + _bdot(mix, t) + add
    t_ref[...] = t


def delta_scan(r, kb, ka, kk, v, g, *, head_dim, reverse, scalar_decay, beta_excl):
    B, T, HW = r.shape
    n_t = T // ROW_TILE
    n_g = HW // V7X_MXU_DIM
    if reverse:
        tile = lambda i: jnp.where(i == 0, 0, n_t - i)
    else:
        tile = lambda i: i
    spec = pl.BlockSpec((1, ROW_TILE, V7X_MXU_DIM), lambda b, h, i: (b, tile(i), h))
    kern = functools.partial(_delta_scan_kernel, head_dim=head_dim, reverse=reverse,
                             scalar_decay=scalar_decay, beta_excl=beta_excl, ka_is_neg_kk=ka is None)
    args = [a for a in (r, kb, ka, kk, v, g) if a is not None]
    return pl.pallas_call(
        kern,
        grid=(B, n_g, n_t),
        in_specs=[spec] * len(args),
        out_specs=spec,
        out_shape=jax.ShapeDtypeStruct((B, T, HW), F32),
        scratch_shapes=[pltpu.VMEM((V7X_MXU_DIM, V7X_MXU_DIM), F32)],
        compiler_params=_params(("parallel", "parallel", "arbitrary")),
        name="delta_scan_rev" if reverse else "delta_scan_fwd",
    )(*args)


def _matmul_kernel(*refs, nk, passes, has_bias, a_silu):
    a_ref, b_ref = refs[0], refs[1]
    bias_ref = refs[2] if has_bias else None
    o_ref = refs[2 + has_bias]
    acc_ref = refs[3 + has_bias] if nk > 1 else None
    a = a_ref[...]
    if a_silu:
        a = a.astype(F32)
        a = a * jax.nn.sigmoid(a)
    if passes == 3:
        part = _dot3(a.astype(F32), b_ref[...].astype(F32))
    else:
        part = _bdot(a, b_ref[...])

    def finish(acc):
        if has_bias:
            acc = acc + bias_ref[...]
        o_ref[...] = acc.astype(o_ref.dtype)

    if nk == 1:
        finish(part)
    else:
        k = pl.program_id(2)

        @pl.when(k == 0)
        def _():
            acc_ref[...] = part

        @pl.when(k > 0)
        def _():
            acc_ref[...] += part

        @pl.when(k == nk - 1)
        def _():
            finish(acc_ref[...])


def matmul(a, b, *, tm, tn, tk=None, bias=None, out_dtype=F32, passes=1, a_silu=False, layer=None,
           name="matmul"):
    M, K = a.shape
    N = b.shape[-1]
    tk = K if tk is None else tk
    assert M % tm == 0 and N % tn == 0 and K % tk == 0, (a.shape, b.shape, tm, tn, tk)
    nk = K // tk
    if layer is None:
        b_spec = pl.BlockSpec((tk, tn), lambda i, j, k: (k, j))
    else:
        b_spec = pl.BlockSpec((pl.Squeezed(), tk, tn), lambda i, j, k: (layer, k, j))
    in_specs = [pl.BlockSpec((tm, tk), lambda i, j, k: (i, k)), b_spec]
    args = [a, b]
    if bias is not None:
        in_specs.append(pl.BlockSpec((1, tn), lambda i, j, k: (0, j)))
        args.append(bias.reshape(1, N).astype(F32))
    kern = functools.partial(_matmul_kernel, nk=nk, passes=passes, has_bias=bias is not None, a_silu=a_silu)
    return pl.pallas_call(
        kern,
        grid=(M // tm, N // tn, nk),
        in_specs=in_specs,
        out_specs=pl.BlockSpec((tm, tn), lambda i, j, k: (i, j)),
        out_shape=jax.ShapeDtypeStruct((M, N), out_dtype),
        scratch_shapes=[pltpu.VMEM((tm, tn), F32)] if nk > 1 else [],
        compiler_params=_params(("parallel", "parallel", "arbitrary")),
        name=name,
    )(*args)


def _norm_mod_kernel(x_ref, g_ref, mod_ref, *o_refs, which):
    x = x_ref[0]
    y = x * lax.rsqrt(jnp.mean(x * x, axis=-1, keepdims=True) + NORM_EPS) * g_ref[...]
    shift = mod_ref[0, 3 * which:3 * which + 1, :]
    scale = mod_ref[0, 3 * which + 1:3 * which + 2, :]
    h = y * (1.0 + scale) + shift
    for o_ref in o_refs:
        o_ref[0] = h.astype(o_ref.dtype)


def norm_mod(x, g, mod, *, which, ctx_tiles, out_dtypes=(BF16,)):
    B, T, D = x.shape
    ctx_row = mod.shape[0] - 1
    sel = lambda b, j: jnp.where(j < ctx_tiles, ctx_row, b)
    tile = pl.BlockSpec((1, ROW_TILE, D), lambda b, j: (b, j, 0))
    outs = pl.pallas_call(
        functools.partial(_norm_mod_kernel, which=which),
        grid=(B, T // ROW_TILE),
        in_specs=[tile,
                  pl.BlockSpec((1, D), lambda b, j: (0, 0)),
                  pl.BlockSpec((1, 6, D), lambda b, j: (sel(b, j), 0, 0))],
        out_specs=[tile] * len(out_dtypes),
        out_shape=[jax.ShapeDtypeStruct((B, T, D), dt) for dt in out_dtypes],
        compiler_params=_params(("parallel", "parallel")),
        name="norm_mod",
    )(x, g.reshape(1, D), mod)
    return outs[0] if len(out_dtypes) == 1 else outs


def _route_kernel(t_ref, wt_ref, bias_ref, gates_ref):
    E = wt_ref.shape[0]
    per = E // N_GROUPS
    th, tl = _split2(t_ref[...])
    wh, wl = _split2(wt_ref[...])
    nt = lambda a, b: lax.dot_general(a, b, (((1,), (1,)), ((), ())), preferred_element_type=F32)
    logits = nt(wh, th) + (nt(wh, tl) + nt(wl, th))
    s = jax.nn.sigmoid(logits)
    sel = s + bias_ref[...]
    srow = [s[e:e + 1, :] for e in range(E)]
    row = [sel[e:e + 1, :] for e in range(E)]
    best_g, best_i = None, None
    for gq in range(N_GROUPS):
        a, b, c, d = row[gq * per:(gq + 1) * per]
        m1, n1, m2, n2 = jnp.maximum(a, b), jnp.minimum(a, b), jnp.maximum(c, d), jnp.minimum(c, d)
        score = jnp.maximum(m1, m2) + jnp.maximum(jnp.minimum(m1, m2), jnp.maximum(n1, n2))
        if gq == 0:
            best_g, best_i = score, jnp.zeros_like(score, dtype=jnp.int32)
        else:
            better = score > best_g
            best_g = jnp.where(better, score, best_g)
            best_i = jnp.where(better, gq, best_i)
    masked = [jnp.where(best_i == e // per, row[e], NEG_INF) for e in range(E)]

    def argmax_first(vals):
        top, idx = vals[0], jnp.zeros_like(best_i)
        for e in range(1, E):
            better = vals[e] > top
            top = jnp.where(better, vals[e], top)
            idx = jnp.where(better, e, idx)
        return idx

    i1 = argmax_first(masked)
    i2 = argmax_first([jnp.where(i1 == e, -jnp.inf, masked[e]) for e in range(E)])
    w1 = sum(jnp.where(i1 == e, srow[e], 0.0) for e in range(E))
    w2 = sum(jnp.where(i2 == e, srow[e], 0.0) for e in range(E))
    tot = w1 + w2
    g1, g2 = w1 / tot, w2 / tot
    gates_ref[...] = jnp.concatenate(
        [jnp.where(i1 == e, g1, 0.0) + jnp.where(i2 == e, g2, 0.0) for e in range(E)], axis=0)


def moe_route(t, router_w, router_bias, *, tm=512):
    n, D = t.shape
    E = router_w.shape[1]
    return pl.pallas_call(
        _route_kernel,
        grid=(n // tm,),
        in_specs=[pl.BlockSpec((tm, D), lambda i: (i, 0)),
                  pl.BlockSpec((E, D), lambda i: (0, 0)),
                  pl.BlockSpec((E, 1), lambda i: (0, 0))],
        out_specs=pl.BlockSpec((E, tm), lambda i: (0, i)),
        out_shape=jax.ShapeDtypeStruct((E, n), F32),
        compiler_params=_params(("parallel",)),
        name="moe_route",
    )(t, router_w.T, router_bias.reshape(E, 1))


def _na_kernel(rpb_ref, q_ref, k_ref, v_ref, kc_ref, vc_ref, o_ref, tb_ref):
    h = pl.program_id(1)
    GW, HD = GRID_W, V7X_LANES
    n_rows = q_ref.shape[1] // GW
    wr = NA_WIN_R
    n_dc = 2 * NA_WIN_C - 1
    qcol = lax.broadcasted_iota(jnp.int32, (GW, 2 * GW), 0)
    lane = lax.broadcasted_iota(jnp.int32, (GW, 2 * GW), 1)
    kcol = lane & (GW - 1)
    upper = lane >= GW
    dci = jnp.clip(kcol - qcol + NA_WIN_C - 1, 0, n_dc - 1)
    cstart = jnp.clip(qcol - NA_WIN_C // 2, 0, GW - NA_WIN_C)
    col_ok = (kcol >= cstart) & (kcol < cstart + NA_WIN_C)
    for dr in range(2 * wr - 2):
        acc = jnp.zeros((GW, 2 * GW), F32)
        for j in range(n_dc):
            lo = rpb_ref[h, dr * n_dc + j]
            hi = rpb_ref[h, (dr + 1) * n_dc + j]
            acc = jnp.where(dci == j, jnp.where(upper, hi, lo), acc)
        tb_ref[dr] = acc
    ok4 = jnp.concatenate([col_ok] * (wr // 2), axis=1)
    kc = kc_ref[0]
    vc = vc_ref[0]
    scale = HD ** -0.5

    def body(r, carry):
        rs = jnp.clip(r - wr // 2, 0, n_rows - wr)
        q = q_ref[0, pl.ds(pl.multiple_of(r * GW, GW), GW), :] * scale
        k0 = pl.multiple_of(rs * GW, GW)
        kw = k_ref[0, pl.ds(k0, wr * GW), :]
        vw = v_ref[0, pl.ds(k0, wr * GW), :]
        dr0 = rs - r + wr - 1
        bias = jnp.concatenate([tb_ref[dr0 + 2 * w] for w in range(wr // 2)], axis=1)
        s_lat = jnp.where(ok4, _bdot_nt(q, kw) + bias, NEG_INF)
        s_ctx = _bdot_nt(q, kc)
        m = jnp.maximum(jnp.max(s_lat, axis=-1, keepdims=True), jnp.max(s_ctx, axis=-1, keepdims=True))
        p_lat = jnp.exp(s_lat - m)
        p_ctx = jnp.exp(s_ctx - m)
        den = jnp.sum(p_lat, axis=-1, keepdims=True) + jnp.sum(p_ctx, axis=-1, keepdims=True)
        o = (_bdot(p_lat, vw) + _bdot(p_ctx, vc)) / den
        o_ref[0, pl.ds(pl.multiple_of(r * GW, GW), GW), :] = o
        return carry

    lax.fori_loop(0, n_rows, body, 0)


def neighbourhood_attention(pl_lat, kv_ctx, rpb):
    B, S, _ = pl_lat.shape
    Lc = kv_ctx.shape[1]
    H, HD = NA_HEADS, V7X_LANES
    lat = lambda off: pl.BlockSpec((1, S, HD), lambda b, h: (b, 0, off + h))
    cx = lambda off: pl.BlockSpec((1, Lc, HD), lambda b, h: (b, 0, off + h))
    return pl.pallas_call(
        _na_kernel,
        grid=(B, H),
        in_specs=[pl.BlockSpec(memory_space=pltpu.SMEM), lat(0), lat(H), lat(2 * H), cx(0), cx(H)],
        out_specs=pl.BlockSpec((1, S, HD), lambda b, h: (b, 0, h)),
        out_shape=jax.ShapeDtypeStruct((B, S, H * HD), F32),
        scratch_shapes=[pltpu.VMEM((2 * NA_WIN_R - 2, GRID_W, 2 * GRID_W), F32)],
        compiler_params=_params(("parallel", "parallel")),
        name="neighbourhood_attention",
    )(rpb.reshape(H, -1), pl_lat, pl_lat, pl_lat, kv_ctx, kv_ctx)


DFT_N1 = 64
DFT_N2 = 128
DFT_SUB = 8


@functools.lru_cache(maxsize=None)
def _dft_constants():
    n1, n2 = DFT_N1, DFT_N2
    n = n1 * n2
    a = np.arange(n1)
    f1 = np.exp(-2j * np.pi * np.outer(a, a) / n1)
    half = f1[:, :n1 // 2]
    eye = np.eye(DFT_SUB)
    fwd = np.kron(half, eye)
    inv = np.kron(np.conj(f1)[:n1 // 2, :], eye) / n
    b = np.arange(n2)
    f2 = np.exp(-2j * np.pi * np.outer(b, b) / n2)
    tw = np.exp(-2j * np.pi * np.outer(a, b) / n)
    m = f2[None, :, :] * tw[:, None, :]
    mt = np.transpose(m, (0, 2, 1))
    c = lambda z: (np.ascontiguousarray(z.real, dtype=np.float32), np.ascontiguousarray(z.imag, dtype=np.float32))
    return c(fwd) + c(inv) + c(m) + c(mt)


def _hyena_conv_kernel(hf_ref, hb_ref, z_ref, fwr_ref, fwi_ref, ivr_ref, ivi_ref,
                       mr_ref, mi_ref, mtr_ref, mti_ref, o_ref, ar_ref, ai_ref, kr_ref, ki_ref):
    L, TC = hf_ref.shape
    N1, N2, SUB = DFT_N1, DFT_N2, DFT_SUB
    nblk = N2 // SUB

    def stage1(load):
        def blk(nb, carry):
            off = pl.multiple_of(nb * SUB, SUB)
            xs = jnp.concatenate([load(n1 * N2 + off) for n1 in range(N1 // 2)], axis=0)
            yr = _bdot(fwr_ref[...], xs)
            yi = _bdot(fwi_ref[...], xs)
            for k1 in range(N1):
                ar_ref[pl.ds(k1 * N2 + off, SUB), :] = yr[k1 * SUB:(k1 + 1) * SUB]
                ai_ref[pl.ds(k1 * N2 + off, SUB), :] = yi[k1 * SUB:(k1 + 1) * SUB]
            return carry

        lax.fori_loop(0, nblk, blk, 0)

    def spectrum(k1):
        rows = pl.ds(pl.multiple_of(k1 * N2, N2), N2)
        a_r, a_i = ar_ref[rows, :], ai_ref[rows, :]
        m_r, m_i = mr_ref[k1], mi_ref[k1]
        return rows, _bdot(m_r, a_r) - _bdot(m_i, a_i), _bdot(m_r, a_i) + _bdot(m_i, a_r)

    stage1(lambda r0: hf_ref[pl.ds(r0, SUB), :])

    def filt_f(k1, carry):
        rows, x_r, x_i = spectrum(k1)
        kr_ref[rows, :] = x_r
        ki_ref[rows, :] = x_i
        return carry

    lax.fori_loop(0, N1, filt_f, 0)
    sub_iota = lax.broadcasted_iota(jnp.int32, (SUB, TC), 0)
    stage1(lambda r0: jnp.where(sub_iota + r0 == 0, 0.0, hb_ref[pl.ds(r0, SUB), :]))

    def filt_b(k1, carry):
        rows, x_r, x_i = spectrum(k1)
        kr_ref[rows, :] += x_r
        ki_ref[rows, :] -= x_i
        return carry

    lax.fori_loop(0, N1, filt_b, 0)

    for b in range(z_ref.shape[0]):
        stage1(lambda r0: z_ref[b, pl.ds(r0, SUB), :])

        def conv(k1, carry):
            rows, x_r, x_i = spectrum(k1)
            f_r, f_i = kr_ref[rows, :], ki_ref[rows, :]
            y_r = x_r * f_r - x_i * f_i
            y_i = x_r * f_i + x_i * f_r
            t_r, t_i = mtr_ref[k1], mti_ref[k1]
            ar_ref[rows, :] = _bdot(t_r, y_r) + _bdot(t_i, y_i)
            ai_ref[rows, :] = _bdot(t_r, y_i) - _bdot(t_i, y_r)
            return carry

        lax.fori_loop(0, N1, conv, 0)
        def last(nb, carry):
            off = pl.multiple_of(nb * SUB, SUB)
            br = jnp.concatenate([ar_ref[pl.ds(k1 * N2 + off, SUB), :] for k1 in range(N1)], axis=0)
            bi = jnp.concatenate([ai_ref[pl.ds(k1 * N2 + off, SUB), :] for k1 in range(N1)], axis=0)
            y = _bdot(ivr_ref[...], br) - _bdot(ivi_ref[...], bi)
            for n1 in range(N1 // 2):
                o_ref[b, pl.ds(n1 * N2 + off, SUB), :] = y[n1 * SUB:(n1 + 1) * SUB]
            return carry

        lax.fori_loop(0, nblk, last, 0)


def hyena_long_conv(z, h_f, h_b, *, tc=V7X_LANES):
    B, L, C = z.shape
    assert L == DFT_N1 * DFT_N2 // 2
    consts = [jnp.asarray(a, BF16) for a in _dft_constants()]
    whole = lambda a: pl.BlockSpec(a.shape, lambda j: (0,) * a.ndim)
    col = pl.BlockSpec((L, tc), lambda j: (0, j))
    bcol = pl.BlockSpec((B, L, tc), lambda j: (0, 0, j))
    return pl.pallas_call(
        _hyena_conv_kernel,
        grid=(C // tc,),
        in_specs=[col, col, bcol] + [whole(a) for a in consts],
        out_specs=bcol,
        out_shape=jax.ShapeDtypeStruct((B, L, C), F32),
        scratch_shapes=[pltpu.VMEM((2 * L, tc), F32)] * 4,
        compiler_params=_params(("parallel",)),
        name="hyena_long_conv",
    )(h_f, h_b, z, *consts)


def _moe_up_kernel(x_ref, wg_ref, wu_ref, gate_ref, o_ref):
    x = x_ref[...]
    hg = _bdot(x, wg_ref[...])
    hu = _bdot(x, wu_ref[...])
    o_ref[...] = (hg * jax.nn.sigmoid(hg) * hu * gate_ref[0]).astype(o_ref.dtype)


def moe_up(t, w_gate, w_up, gates_t, *, layer, tm, tf=256):
    n, D = t.shape
    _, E, _, F = w_gate.shape
    nf = F // tf
    wspec = pl.BlockSpec((pl.Squeezed(), pl.Squeezed(), D, tf), lambda i, e, j: (layer, e, 0, j))
    return pl.pallas_call(
        _moe_up_kernel,
        grid=(n // tm, E, nf),
        in_specs=[pl.BlockSpec((tm, D), lambda i, e, j: (i, 0)), wspec, wspec,
                  pl.BlockSpec((1, tm, 1), lambda i, e, j: (e, i, 0))],
        out_specs=pl.BlockSpec((tm, tf), lambda i, e, j: (i, e * nf + j)),
        out_shape=jax.ShapeDtypeStruct((n, E * F), BF16),
        compiler_params=_params(("parallel", "arbitrary", "arbitrary")),
        name="moe_up",
    )(t, w_gate, w_up, gates_t)


def _pick(n, options):
    for t in options:
        if n % t == 0:
            return t
    raise ValueError(f"no tile for {n} in {options}")


def _mm(a, w, **kw):
    tm = _pick(a.shape[0], (1088, 1024, 768, 512, 256, 8))
    tn = _pick(w.shape[1], (512, 640, 384, 256, 128))
    return matmul(a, w, tm=tm, tn=tn, **kw)


def _pad_to(a, axis, mult):
    pad = (-a.shape[axis]) % mult
    if pad == 0:
        return a
    widths = [(0, 0)] * a.ndim
    widths[axis] = (0, pad)
    return jnp.pad(a, widths)


def _silu(x):
    return x * jax.nn.sigmoid(x)


def _seq_parts(x, lc, fn):
    if lc == 0:
        return fn(x)
    return jnp.concatenate([fn(x[:, :lc]), fn(x[:, lc:])], axis=1)


def _dwconv(x, w):
    k, length = w.shape[0], x.shape[1]
    xp = jnp.pad(x, ((0, 0), (k // 2, k // 2), (0, 0)))
    return sum(xp[:, j:j + length] * w[j] for j in range(k))


def _centred_shift(p):
    prev = jnp.pad(p, ((0, 0), (1, 0), (0, 0)))[:, :-1]
    nxt = jnp.pad(p, ((0, 0), (0, 1), (0, 0)))[:, 1:]
    return 0.5 * (prev + nxt)


def _l2norm(t):
    return t * lax.rsqrt(jnp.sum(t * t, -1, keepdims=True) + 1e-6)


def _axial_rope(t):
    length, d = t.shape[1], t.shape[-1]
    nf = d // 4
    pos = jnp.arange(length)
    inv = ROPE_BASE ** (-jnp.arange(nf, dtype=F32) / nf)

    def cs(p):
        ang = p.astype(F32)[:, None] * inv[None, :]
        return jnp.cos(ang), jnp.sin(ang)

    cr, sr = cs(pos // GRID_W)
    cc, sc = cs(pos % GRID_W)
    cos = jnp.concatenate([cr, cr, cc, cc], -1)[None, :, None, :]
    sin = jnp.concatenate([-sr, sr, -sc, sc], -1)[None, :, None, :]
    swapped = jnp.flip(t.reshape(*t.shape[:-1], 2, 2, nf), axis=-2).reshape(t.shape)
    return t * cos + swapped * sin


def _mixer_ab(h, lc, w_in, dn_conv_w, dn_a_log, dn_dt_bias, dn_norm_w,
              rw_mu, rw_w0, rw_w2, rw_a0, rw_a2, rw_g2, rw_k_k, rw_k_a, rw_r_k, rw_ln_w, rw_ln_b, w_out):
    B, T, D = h.shape
    mw = D // 2
    hd_dn = mw // DN_HEADS
    h_rw = mw // RW_HD
    p_a = 4 * mw + 4 * DN_HEADS
    h2 = h.reshape(B * T, D)
    pa = _mm(h2, _pad_to(w_in[:, :p_a], 1, V7X_LANES).astype(BF16)).reshape(B, T, -1)
    pb = _mm(h2, _pad_to(w_in[:, p_a:], 1, V7X_LANES).astype(BF16)).reshape(B, T, -1)

    qkv = _seq_parts(pa[..., :3 * mw], lc, lambda u: _silu(_dwconv(u, dn_conv_w)))
    q, k, v = (t.reshape(B, T, DN_HEADS, hd_dn) for t in jnp.split(qkv, 3, axis=-1))
    rope_lat = lambda t: jnp.concatenate([t[:, :lc], _axial_rope(t[:, lc:])], axis=1)
    q, k = rope_lat(_l2norm(q)), rope_lat(_l2norm(k))
    q = q * hd_dn ** -0.5
    z = pa[..., 3 * mw:4 * mw]
    bg = pa[..., 4 * mw:p_a].reshape(B, T, 4, DN_HEADS)
    beta = jax.nn.sigmoid(bg[:, :, :2])
    g = -jnp.exp(dn_a_log) * jax.nn.softplus(bg[:, :, 2:] + dn_dt_bias)
    flat = lambda t: t.reshape(B, T, mw)
    o = 0.0
    for d in range(2):
        bk = k * beta[:, :, d, :, None]
        gd = jnp.broadcast_to(g[:, :, d, :, None], k.shape)
        o = o + delta_scan(flat(q), flat(k), None, flat(bk), flat(v), flat(gd),
                           head_dim=hd_dn, reverse=d == 1, scalar_decay=True, beta_excl=False)
    o = o.reshape(B, T, DN_HEADS, hd_dn)
    o = o * lax.rsqrt(jnp.mean(o * o, -1, keepdims=True) + NORM_EPS) * dn_norm_w
    dn = flat(o * _silu(z).reshape(o.shape))

    p_b = w_in.shape[1] - p_a
    p = pb[..., :p_b]
    p = p + (_seq_parts(p, lc, _centred_shift) - p) * rw_mu
    r, k, v = p[..., :mw], p[..., mw:2 * mw], p[..., 2 * mw:3 * mw]
    o1 = 3 * mw
    n_w, n_a = rw_w2.shape[1], rw_a2.shape[1]
    o2 = o1 + 2 * n_w
    o3 = o2 + 2 * n_a
    wl = p[..., o1:o2].reshape(B * T, 2, n_w)
    al = p[..., o2:o3].reshape(B * T, 2, n_a)
    gl = p[..., o3:].reshape(B * T, -1)
    gate = _mm(_pad_to(jax.nn.sigmoid(gl), 1, V7X_LANES), _pad_to(rw_g2, 0, V7X_LANES)).reshape(B, T, mw)
    heads = lambda t: t.reshape(B, T, h_rw, RW_HD)
    kk = flat(_l2norm(heads(k * rw_k_k)))
    y = 0.0
    for d in range(2):
        w = -jax.nn.softplus(-(rw_w0[d] + _mm(jnp.tanh(wl[:, d]), rw_w2[d]).reshape(B, T, mw))) - 0.5
        a = jax.nn.sigmoid(rw_a0[d] + _mm(al[:, d], rw_a2[d]).reshape(B, T, mw))
        kd = k * (1.0 + (a - 1.0) * rw_k_a)
        y = y + delta_scan(r, kk, -(kk * a), kd, v, -jnp.exp(w),
                           head_dim=RW_HD, reverse=d == 1, scalar_decay=False, beta_excl=True)
    y = heads(y)
    mu = jnp.mean(y, -1, keepdims=True)
    var = jnp.mean(jnp.square(y - mu), -1, keepdims=True)
    yn = flat((y - mu) * lax.rsqrt(var + RW_LN_EPS)) * rw_ln_w + rw_ln_b
    bonus = flat(jnp.sum(heads(r) * heads(k) * rw_r_k, -1, keepdims=True) * heads(v))
    rw = (yn + bonus) * gate

    cat = jnp.concatenate([dn, rw], -1).astype(BF16).reshape(B * T, D)
    return _mm(cat, w_out.astype(BF16)).reshape(B, T, D)


def _hyena_filters(length, ch, w1, b1, freq, w2, b2, w3, b3, w4):
    t = jnp.linspace(0.0, 1.0, length, dtype=F32)
    bands = (HY_EMB - 1) // 2
    wpos = 2 * math.pi * jnp.arange(length, dtype=F32) / length
    fb = jnp.linspace(1e-4, bands - 1, bands, dtype=F32)
    ang = wpos[:, None] * fb[None, :]
    z = jnp.concatenate([t[:, None], jnp.cos(ang), -jnp.sin(ang)], axis=-1)
    dense = lambda u, w: _mm(_pad_to(u, 1, V7X_LANES), _pad_to(_pad_to(w, 0, V7X_LANES), 1, V7X_LANES),
                             passes=3)[:, :w.shape[1]]
    h = jnp.sin(freq * (dense(z, w1) + b1))
    h = jnp.sin(freq * (dense(h, w2) + b2))
    h = jnp.sin(freq * (dense(h, w3) + b3))
    h = dense(h, w4).reshape(length, 2, ch)
    deltas = jnp.abs(jnp.linspace(HY_MIN_DECAY, HY_MAX_DECAY, ch, dtype=F32))
    window = jnp.exp(-t[:, None] * deltas[None, :])
    return h[:, 0] * window, h[:, 1] * window


def _mixer_cd(hc, hl, w_in, na_rpb, hy_conv_w, hy_conv_b, filt, hy_skip, w_out):
    B, S, D = hl.shape
    lc = hc.shape[1]
    mw = D // 2
    w_bf = w_in.astype(BF16)
    pl_ = _mm(hl.reshape(B * S, D), w_bf).reshape(B, S, -1)
    kv_c = _mm(hc.reshape(B * lc, D), w_bf[:, mw:3 * mw]).reshape(B, lc, 2 * mw)
    att = neighbourhood_attention(pl_, kv_c, na_rpb)
    u = _dwconv(pl_[..., 3 * mw:], hy_conv_w) + hy_conv_b
    x0, x1, v = jnp.split(u, 3, axis=-1)
    h_f, h_b = _hyena_filters(S, mw, *filt)
    z = x1 * v
    hy = x0 * (hyena_long_conv(z, h_f, h_b) + z * hy_skip)
    cat = jnp.concatenate([att, hy], -1).astype(BF16).reshape(B * S, D)
    return _mm(cat, w_out.astype(BF16)).reshape(B, S, D)


def _moe(t_bf, t_f32, router_w, router_bias, w_gate, w_up, w_down, layer):
    n, D = t_bf.shape
    _, E, _, F = w_gate.shape
    gates_t = moe_route(t_f32, router_w, router_bias)
    tm = _pick(n, (1088, 1024, 768, 512, 256))
    act = moe_up(t_bf, w_gate, w_up, gates_t[:, :, None], layer=layer, tm=tm)
    return matmul(act, w_down.reshape(-1, E * F, D), tm=tm, tn=1024, tk=1024, layer=layer, name="moe_down")


def kernel(x, c, ctx, c_ctx, ada_w, ada_b, norm1_g, norm2_g, final_g, ab_w_in, dn_conv_w, dn_a_log, dn_dt_bias, dn_norm_w, rw_mu, rw_w0, rw_w2, rw_a0, rw_a2, rw_g2, rw_k_k, rw_k_a, rw_r_k, rw_ln_w, rw_ln_b, ab_w_out, cd_w_in, na_rpb, hy_conv_w, hy_conv_b, hy_w1, hy_b1, hy_freq, hy_w2, hy_b2, hy_w3, hy_b3, hy_w4, hy_skip, cd_w_out, router_w, router_bias, moe_w_gate, moe_w_up, moe_w_down):
    B, S, D = x.shape
    lc = ctx.shape[1]
    assert ada_w.shape[0] == 2 and lc == ROW_TILE and S % ROW_TILE == 0 and B + 1 <= V7X_SUBLANES
    T = lc + S
    ctx_tiles = lc // ROW_TILE
    cond = jnp.concatenate([c, c_ctx[None], jnp.zeros((V7X_SUBLANES - B - 1, D), F32)], axis=0)
    mods = [matmul(cond, ada_w, tm=V7X_SUBLANES, tn=512, bias=ada_b[l], a_silu=True, layer=l,
                   name="adaln")[:B + 1].reshape(B + 1, 6, D) for l in range(2)]

    def rows(m, idx):
        return jnp.concatenate([jnp.broadcast_to(m[B, idx], (B, lc, D)),
                                jnp.broadcast_to(m[:B, idx][:, None, :], (B, S, D))], axis=1)

    m = mods[0]
    xa = jnp.concatenate([ctx, x], axis=1)
    h = norm_mod(xa, norm1_g[0], m, which=0, ctx_tiles=ctx_tiles)
    mix = _mixer_ab(h, lc, ab_w_in[0], dn_conv_w[0], dn_a_log[0], dn_dt_bias[0], dn_norm_w[0],
                    rw_mu[0], rw_w0[0], rw_w2[0], rw_a0[0], rw_a2[0], rw_g2[0], rw_k_k[0], rw_k_a[0],
                    rw_r_k[0], rw_ln_w[0], rw_ln_b[0], ab_w_out[0])
    xa = xa + rows(m, 2) * mix
    h_bf, h_f32 = norm_mod(xa, norm2_g[0], m, which=1, ctx_tiles=ctx_tiles, out_dtypes=(BF16, F32))
    f = _moe(h_bf.reshape(B * T, D), h_f32.reshape(B * T, D), router_w, router_bias,
             moe_w_gate, moe_w_up, moe_w_down, 0).reshape(B, T, D)
    xa = xa + rows(m, 5) * f

    m = mods[1]
    cx, xl = xa[:, :lc], xa[:, lc:]
    hl = norm_mod(xl, norm1_g[1], m, which=0, ctx_tiles=0)
    hc = norm_mod(cx, norm1_g[1], m, which=0, ctx_tiles=ctx_tiles)
    filt = (hy_w1[0], hy_b1[0], hy_freq[0], hy_w2[0], hy_b2[0], hy_w3[0], hy_b3[0], hy_w4[0])
    ml = _mixer_cd(hc, hl, cd_w_in[0], na_rpb[0], hy_conv_w[0], hy_conv_b[0], filt, hy_skip[0], cd_w_out[0])
    xl = xl + m[:B, 2][:, None, :] * ml
    h_bf, h_f32 = norm_mod(xl, norm2_g[1], m, which=1, ctx_tiles=0, out_dtypes=(BF16, F32))
    f = _moe(h_bf.reshape(B * S, D), h_f32.reshape(B * S, D), router_w, router_bias,
             moe_w_gate, moe_w_up, moe_w_down, 1).reshape(B, S, D)
    xl = xl + m[:B, 5][:, None, :] * f
    return xl * lax.rsqrt(jnp.mean(xl * xl, -1, keepdims=True) + NORM_EPS) * final_g
```

```python
import functools
import math

import jax
import jax.numpy as jnp
import numpy as np
from jax import lax
from jax.experimental import pallas as pl
from jax.experimental.pallas import tpu as pltpu

F32 = jnp.float32
BF16 = jnp.bfloat16

V7X_LANES = 128
V7X_SUBLANES = 8
V7X_MXU_DIM = 256
V7X_VMEM_LIMIT_BYTES = 56 * 1024 * 1024

GRID_W = 64
NORM_EPS = 1e-6
NEG_INF = -1e30
ROPE_BASE = 10000.0
DN_HEADS = 16
DN_CONV = 5
RW_HD = 64
RW_LN_EPS = 64e-5
NA_HEADS = 16
NA_WIN_R = 8
NA_WIN_C = 16
HY_EMB = 33
HY_TARGET = 1e-2
HY_MAX_DECAY = math.log(HY_TARGET) / 0.3
HY_MIN_DECAY = math.log(HY_TARGET) / 1.5
N_EXPERTS = 16
N_GROUPS = 4

ROW_TILE = 256
SCAN_CHUNK = 64


def _params(semantics, vmem=V7X_VMEM_LIMIT_BYTES):
    return pltpu.CompilerParams(dimension_semantics=semantics, vmem_limit_bytes=vmem)


def _bdot(a, b):
    return jnp.dot(a.astype(BF16), b.astype(BF16), preferred_element_type=F32)


def _bdot_nt(a, b):
    return lax.dot_general(a.astype(BF16), b.astype(BF16), (((1,), (1,)), ((), ())),
                           preferred_element_type=F32)


def _split2(x):
    hi = x.astype(BF16)
    lo = (x - hi.astype(F32)).astype(BF16)
    return hi, lo


def _split3(x):
    hi = x.astype(BF16)
    r1 = x - hi.astype(F32)
    mid = r1.astype(BF16)
    lo = (r1 - mid.astype(F32)).astype(BF16)
    return hi, mid, lo


def _dot3(a, b):
    ah, al = _split2(a)
    bh, bl = _split2(b)
    d = lambda u, v: jnp.dot(u, v, preferred_element_type=F32)
    return d(ah, bh) + (d(ah, bl) + d(al, bh))


def _dot_exact_left(m, x):
    hi, mid, lo = _split3(x)
    d = lambda v: jnp.dot(m, v, preferred_element_type=F32)
    return d(hi) + (d(mid) + d(lo))


def _unit_tri_inverse(a, row, col, mm):
    eye = (row == col).astype(F32)
    a8 = jnp.where((row >> 3) == (col >> 3), a, 0.0)
    a2 = mm(a8, a8)
    p = eye + a8
    p = p + mm(p, a2)
    a4 = mm(a2, a2)
    x = p + mm(p, a4)
    for sh in (3, 4, 5):
        m = ((row >> (sh + 1)) == (col >> (sh + 1))) & ((row >> sh) != (col >> sh))
        x = x + mm(mm(x, jnp.where(m, a, 0.0)), x)
    return x


def _delta_scan_kernel(*refs, head_dim, reverse, scalar_decay, beta_excl, ka_is_neg_kk):
    if ka_is_neg_kk:
        r_ref, kb_ref, kk_ref, v_ref, g_ref, y_ref, t_ref = refs
        ka_ref = None
    else:
        r_ref, kb_ref, ka_ref, kk_ref, v_ref, g_ref, y_ref, t_ref = refs
    TT, W, C = ROW_TILE, V7X_MXU_DIM, SCAN_CHUNK
    n_heads = W // head_dim
    hshift = int(math.log2(head_dim))

    @pl.when(pl.program_id(2) == 0)
    def _():
        t_ref[...] = jnp.zeros_like(t_ref)

    row = lax.broadcasted_iota(jnp.int32, (TT, TT), 0)
    col = lax.broadcasted_iota(jnp.int32, (TT, TT), 1)
    same = (row >> 6) == (col >> 6)
    if reverse:
        strict, incl = same & (col > row), same & (col >= row)
    else:
        strict, incl = same & (col < row), same & (col <= row)

    g = g_ref[0]
    gi = _dot_exact_left(incl.astype(BF16), g)
    gc = _dot_exact_left(same.astype(BF16), g)
    gb = gi - g if beta_excl else gi
    r, kb, kk, v = r_ref[0], kb_ref[0], kk_ref[0], v_ref[0]
    ka = -kk if ka_is_neg_kk else ka_ref[0]
    e_rest = jnp.exp(gc - gi)
    r_s = r * jnp.exp(gi)
    kb_s = kb * jnp.exp(gb)
    ka_e = ka * e_rest
    kk_e = kk * e_rest
    if scalar_decay:
        assert head_dim == V7X_LANES
        gram_r, gram_b, gram_a, gram_k = r, kb, ka, kk
    else:
        inv = jnp.exp(-gi)
        gram_r, gram_b, gram_a, gram_k = r_s, kb_s, ka * inv, kk * inv

    lane_head = lax.broadcasted_iota(jnp.int32, (TT, W), 1) >> hshift
    b_hat = jnp.zeros((TT, W), F32)
    w1 = jnp.zeros((TT, W), F32)
    r_hat = r_s
    y0 = jnp.zeros((TT, W), F32)
    for j in range(n_heads):
        mj = lane_head == j
        bl = jnp.where(mj, gram_b, 0.0)
        rl = jnp.where(mj, gram_r, 0.0)
        a_bk, a_rk = _bdot_nt(bl, gram_k), _bdot_nt(rl, gram_k)
        if ka_is_neg_kk:
            a_ba, a_ra = -a_bk, -a_rk
        else:
            a_ba, a_ra = _bdot_nt(bl, gram_a), _bdot_nt(rl, gram_a)
        if scalar_decay:
            sl = slice(j * head_dim, (j + 1) * head_dim)
            gi_m = jnp.concatenate([gi[:, sl]] * n_heads, axis=1)
            gb_m = jnp.concatenate([gb[:, sl]] * n_heads, axis=1)
            gi_t = gi_m.T
            d_b = jnp.exp(jnp.minimum(gb_m - gi_t, 0.0))
            d_r = jnp.exp(jnp.minimum(gi_m - gi_t, 0.0))
            a_ba, a_bk, a_ra, a_rk = a_ba * d_b, a_bk * d_b, a_ra * d_r, a_rk * d_r
        a_ba = jnp.where(strict, a_ba, 0.0)
        a_bk = jnp.where(strict, a_bk, 0.0)
        a_ra = jnp.where(incl, a_ra, 0.0)
        a_rk = jnp.where(incl, a_rk, 0.0)
        x = _unit_tri_inverse(a_ba, row, col, _bdot)
        bj = _bdot(x, kb_s)
        wj = _bdot(x, _bdot(a_bk, v))
        b_hat = jnp.where(mj, bj, b_hat)
        w1 = jnp.where(mj, wj, w1)
        r_hat = r_hat + jnp.where(mj, _bdot(a_ra, bj), 0.0)
        y0 = jnp.where(mj, _bdot(a_ra, wj) + _bdot(a_rk, v), y0)

    rw = lax.broadcasted_iota(jnp.int32, (W, W), 0)
    cw = lax.broadcasted_iota(jnp.int32, (W, W), 1)
    blockdiag = (rw >> hshift) == (cw >> hshift)
    eye_w = rw == cw
    t = t_ref[...]
    for c in (range(TT // C - 1, -1, -1) if reverse else range(TT // C)):
        sl = slice(c * C, (c + 1) * C)
        y_ref[0, sl, :] = _bdot(r_hat[sl], t) + y0[sl]
        ka_t = ka_e[sl].T
        kk_t = kk_e[sl].T
        decay_row = jnp.exp(gc[c * C:c * C + 1, :])
        decay_col = jnp.sum(jnp.where(eye_w, decay_row, 0.0), axis=1, keepdims=True)
        mix = jnp.where(blockdiag, _bdot(ka_t, b_hat[sl]), 0.0)
        add = jnp.where(blockdiag, _bdot(ka_t, w1[sl]) + _bdot(kk_t, v[sl]), 0.0)
        t = decay_col * t + _bdot(mix, t) + add
    t_ref[...] = t


def delta_scan(r, kb, ka, kk, v, g, *, head_dim, reverse, scalar_decay, beta_excl):
    B, T, HW = r.shape
    n_t = T // ROW_TILE
    n_g = HW // V7X_MXU_DIM
    if reverse:
        tile = lambda i: jnp.where(i == 0, 0, n_t - i)
    else:
        tile = lambda i: i
    spec = pl.BlockSpec((1, ROW_TILE, V7X_MXU_DIM), lambda b, h, i: (b, tile(i), h))
    kern = functools.partial(_delta_scan_kernel, head_dim=head_dim, reverse=reverse,
                             scalar_decay=scalar_decay, beta_excl=beta_excl, ka_is_neg_kk=ka is None)
    args = [a for a in (r, kb, ka, kk, v, g) if a is not None]
    return pl.pallas_call(
        kern,
        grid=(B, n_g, n_t),
        in_specs=[spec] * len(args),
        out_specs=spec,
        out_shape=jax.ShapeDtypeStruct((B, T, HW), F32),
        scratch_shapes=[pltpu.VMEM((V7X_MXU_DIM, V7X_MXU_DIM), F32)],
        compiler_params=_params(("parallel", "parallel", "arbitrary")),
        name="delta_scan_rev" if reverse else "delta_scan_fwd",
    )(*args)


def _matmul_kernel(*refs, nk, passes, has_bias, a_silu):
    a_ref, b_ref = refs[0], refs[1]
    bias_ref = refs[2] if has_bias else None
    o_ref = refs[2 + has_bias]
    acc_ref = refs[3 + has_bias] if nk > 1 else None
    a = a_ref[...]
    if a_silu:
        a = a.astype(F32)
        a = a * jax.nn.sigmoid(a)
    if passes == 3:
        part = _dot3(a.astype(F32), b_ref[...].astype(F32))
    else:
        part = _bdot(a, b_ref[...])

    def finish(acc):
        if has_bias:
            acc = acc + bias_ref[...]
        o_ref[...] = acc.astype(o_ref.dtype)

    if nk == 1:
        finish(part)
    else:
        k = pl.program_id(2)

        @pl.when(k == 0)
        def _():
            acc_ref[...] = part

        @pl.when(k > 0)
        def _():
            acc_ref[...] += part

        @pl.when(k == nk - 1)
        def _():
            finish(acc_ref[...])


def matmul(a, b, *, tm, tn, tk=None, bias=None, out_dtype=F32, passes=1, a_silu=False, layer=None,
           name="matmul"):
    M, K = a.shape
    N = b.shape[-1]
    tk = K if tk is None else tk
    assert M % tm == 0 and N % tn == 0 and K % tk == 0, (a.shape, b.shape, tm, tn, tk)
    nk = K // tk
    if layer is None:
        b_spec = pl.BlockSpec((tk, tn), lambda i, j, k: (k, j))
    else:
        b_spec = pl.BlockSpec((pl.Squeezed(), tk, tn), lambda i, j, k: (layer, k, j))
    in_specs = [pl.BlockSpec((tm, tk), lambda i, j, k: (i, k)), b_spec]
    args = [a, b]
    if bias is not None:
        in_specs.append(pl.BlockSpec((1, tn), lambda i, j, k: (0, j)))
        args.append(bias.reshape(1, N).astype(F32))
    kern = functools.partial(_matmul_kernel, nk=nk, passes=passes, has_bias=bias is not None, a_silu=a_silu)
    return pl.pallas_call(
        kern,
        grid=(M // tm, N // tn, nk),
        in_specs=in_specs,
        out_specs=pl.BlockSpec((tm, tn), lambda i, j, k: (i, j)),
        out_shape=jax.ShapeDtypeStruct((M, N), out_dtype),
        scratch_shapes=[pltpu.VMEM((tm, tn), F32)] if nk > 1 else [],
        compiler_params=_params(("parallel", "parallel", "arbitrary")),
        name=name,
    )(*args)


def _pack_halves(h):
    half = h.shape[1] // 2
    bits = lax.bitcast_convert_type(h.astype(BF16).astype(F32), jnp.uint32)
    return (bits[:, half:] & jnp.uint32(0xFFFF0000)) | (bits[:, :half] >> 16)


def _unpack_halves(w):
    lo = lax.bitcast_convert_type(w << 16, F32).astype(BF16)
    hi = lax.bitcast_convert_type(w & jnp.uint32(0xFFFF0000), F32).astype(BF16)
    return lo, hi


def _norm_mod_kernel(x_ref, g_ref, mod_ref, *o_refs, which, kinds):
    x = x_ref[0]
    y = x * lax.rsqrt(jnp.mean(x * x, axis=-1, keepdims=True) + NORM_EPS) * g_ref[...]
    shift = mod_ref[0, 3 * which:3 * which + 1, :]
    scale = mod_ref[0, 3 * which + 1:3 * which + 2, :]
    h = y * (1.0 + scale) + shift
    for kind, o_ref in zip(kinds, o_refs):
        o_ref[0] = _pack_halves(h) if kind == "packed" else h.astype(o_ref.dtype)


def norm_mod(x, g, mod, *, which, ctx_tiles, kinds=("bf16",)):
    B, T, D = x.shape
    ctx_row = mod.shape[0] - 1
    sel = lambda b, j: jnp.where(j < ctx_tiles, ctx_row, b)
    tile = lambda w: pl.BlockSpec((1, ROW_TILE, w), lambda b, j: (b, j, 0))
    shapes = {"bf16": (D, BF16), "f32": (D, F32), "packed": (D // 2, jnp.uint32)}
    outs = pl.pallas_call(
        functools.partial(_norm_mod_kernel, which=which, kinds=kinds),
        grid=(B, T // ROW_TILE),
        in_specs=[tile(D),
                  pl.BlockSpec((1, D), lambda b, j: (0, 0)),
                  pl.BlockSpec((1, 6, D), lambda b, j: (sel(b, j), 0, 0))],
        out_specs=[tile(shapes[k][0]) for k in kinds],
        out_shape=[jax.ShapeDtypeStruct((B, T, shapes[k][0]), shapes[k][1]) for k in kinds],
        compiler_params=_params(("parallel", "parallel")),
        name="norm_mod",
    )(x, g.reshape(1, D), mod)
    return outs[0] if len(kinds) == 1 else outs


def _route_kernel(t_ref, wt_ref, bias_ref, idx_ref, gate_ref, cnt_ref, run_ref):
    E = wt_ref.shape[0]
    tm = t_ref.shape[0]
    per = E // N_GROUPS

    @pl.when(pl.program_id(0) == 0)
    def _():
        run_ref[...] = jnp.zeros_like(run_ref)

    th, tl = _split2(t_ref[...])
    wh, wl = _split2(wt_ref[...])
    nt = lambda a, b: lax.dot_general(a, b, (((1,), (1,)), ((), ())), preferred_element_type=F32)
    logits = nt(wh, th) + (nt(wh, tl) + nt(wl, th))
    s = jax.nn.sigmoid(logits)
    sel = s + bias_ref[...]
    srow = [s[e:e + 1, :] for e in range(E)]
    row = [sel[e:e + 1, :] for e in range(E)]
    best_g, best_i = None, None
    for gq in range(N_GROUPS):
        a, b, c, d = row[gq * per:(gq + 1) * per]
        m1, n1, m2, n2 = jnp.maximum(a, b), jnp.minimum(a, b), jnp.maximum(c, d), jnp.minimum(c, d)
        score = jnp.maximum(m1, m2) + jnp.maximum(jnp.minimum(m1, m2), jnp.maximum(n1, n2))
        if gq == 0:
            best_g, best_i = score, jnp.zeros_like(score, dtype=jnp.int32)
        else:
            better = score > best_g
            best_g = jnp.where(better, score, best_g)
            best_i = jnp.where(better, gq, best_i)
    masked = [jnp.where(best_i == e // per, row[e], NEG_INF) for e in range(E)]

    def argmax_first(vals):
        top, idx = vals[0], jnp.zeros_like(best_i)
        for e in range(1, E):
            better = vals[e] > top
            top = jnp.where(better, vals[e], top)
            idx = jnp.where(better, e, idx)
        return idx

    i1 = argmax_first(masked)
    i2 = argmax_first([jnp.where(i1 == e, -jnp.inf, masked[e]) for e in range(E)])
    w1 = sum(jnp.where(i1 == e, srow[e], 0.0) for e in range(E))
    w2 = sum(jnp.where(i2 == e, srow[e], 0.0) for e in range(E))
    tot = w1 + w2
    chosen = jnp.concatenate([jnp.where((i1 == e) | (i2 == e), 1.0, 0.0) for e in range(E)], axis=0)
    earlier = lax.broadcasted_iota(jnp.int32, (tm, tm), 0) < lax.broadcasted_iota(jnp.int32, (tm, tm), 1)
    pos = run_ref[:, 0:1] + _bdot(chosen, earlier.astype(BF16))
    run_ref[...] = run_ref[...] + jnp.sum(chosen, axis=1, keepdims=True)
    p1 = sum(jnp.where(i1 == e, pos[e:e + 1, :], 0.0) for e in range(E)).astype(jnp.int32)
    p2 = sum(jnp.where(i2 == e, pos[e:e + 1, :], 0.0) for e in range(E)).astype(jnp.int32)
    zi = jnp.zeros((V7X_SUBLANES - 4, tm), jnp.int32)
    idx_ref[...] = jnp.concatenate([i1, i2, p1, p2, zi], axis=0)
    zf = jnp.zeros((V7X_SUBLANES - 2, tm), F32)
    gate_ref[...] = jnp.concatenate([w1 / tot, w2 / tot, zf], axis=0)
    cnt_ref[...] = run_ref[...]


def moe_route(t, router_w, router_bias, *, tm=512):
    n, D = t.shape
    E = router_w.shape[1]
    tm = min(tm, n)
    row8 = pl.BlockSpec((V7X_SUBLANES, tm), lambda i: (0, i))
    return pl.pallas_call(
        _route_kernel,
        grid=(n // tm,),
        in_specs=[pl.BlockSpec((tm, D), lambda i: (i, 0)),
                  pl.BlockSpec((E, D), lambda i: (0, 0)),
                  pl.BlockSpec((E, 1), lambda i: (0, 0))],
        out_specs=[row8, row8, pl.BlockSpec((E, V7X_LANES), lambda i: (0, 0))],
        out_shape=[jax.ShapeDtypeStruct((V7X_SUBLANES, n), jnp.int32),
                   jax.ShapeDtypeStruct((V7X_SUBLANES, n), F32),
                   jax.ShapeDtypeStruct((E, V7X_LANES), F32)],
        scratch_shapes=[pltpu.VMEM((E, V7X_LANES), F32)],
        compiler_params=_params(("arbitrary",)),
        name="moe_route",
    )(t, router_w.T, router_bias.reshape(E, 1))


MOE_TILE = 1152
COPY_ROWS = 256


def _dispatch_kernel(slot_ref, x_ref, xg_ref, sem):
    n = slot_ref.shape[0] // 2
    base = pl.program_id(0) * COPY_ROWS

    def copy(r, k):
        return pltpu.make_async_copy(x_ref.at[pl.ds(r, 1)], xg_ref.at[pl.ds(slot_ref[k * n + base + r], 1)], sem)

    def start(r, carry):
        copy(r, 0).start()
        copy(r, 1).start()
        return carry

    def wait(r, carry):
        copy(r, 0).wait()
        copy(r, 1).wait()
        return carry

    lax.fori_loop(0, COPY_ROWS, start, 0)
    lax.fori_loop(0, COPY_ROWS, wait, 0)


def moe_dispatch(slots, x_words, n_rows):
    n, W = x_words.shape
    return pl.pallas_call(
        _dispatch_kernel,
        grid_spec=pltpu.PrefetchScalarGridSpec(
            num_scalar_prefetch=1,
            grid=(n // COPY_ROWS,),
            in_specs=[pl.BlockSpec((COPY_ROWS, W), lambda i, s: (i, 0))],
            out_specs=pl.BlockSpec(memory_space=pl.ANY),
            scratch_shapes=[pltpu.SemaphoreType.DMA(())],
        ),
        out_shape=jax.ShapeDtypeStruct((n_rows, W), x_words.dtype),
        compiler_params=_params(("arbitrary",)),
        name="moe_dispatch",
    )(slots, x_words)


def _moe_up_kernel(te_ref, rows_ref, blk_ref, x_ref, wg_ref, wu_ref, o_ref, xs_ref):
    rows = rows_ref[pl.program_id(0)]
    half = x_ref.shape[1]

    @pl.when((rows > 0) & (pl.program_id(1) == 0))
    def _():
        valid = lax.broadcasted_iota(jnp.int32, x_ref.shape, 0) < rows
        lo, hi = _unpack_halves(jnp.where(valid, x_ref[...], jnp.uint32(0)))
        xs_ref[:, :half] = lo
        xs_ref[:, half:] = hi

    @pl.when(rows > 0)
    def _():
        x = xs_ref[...]
        hg = jnp.dot(x, wg_ref[...].astype(BF16), preferred_element_type=F32)
        hu = jnp.dot(x, wu_ref[...].astype(BF16), preferred_element_type=F32)
        o_ref[...] = (hg * jax.nn.sigmoid(hg) * hu).astype(o_ref.dtype)


def _moe_down_kernel(te_ref, rows_ref, blk_ref, a_ref, wd_ref, o_ref):
    @pl.when(rows_ref[pl.program_id(0)] > 0)
    def _():
        o_ref[...] = _bdot(a_ref[...], wd_ref[...])


def moe_experts(xg, w_gate, w_up, w_down, tile_expert, tile_rows, tile_block, *, layer, tm, tf=256, tn=1024):
    _, E, D, F = w_gate.shape
    nt = tile_expert.shape[0]
    sq = pl.Squeezed()
    nf, nc = F // tf, D // tn
    hold = lambda i, last, tr, j: jnp.where(tr[j] > 0, i, last)
    act = pl.pallas_call(
        _moe_up_kernel,
        grid_spec=pltpu.PrefetchScalarGridSpec(
            num_scalar_prefetch=3,
            grid=(nt, nf),
            in_specs=[pl.BlockSpec((tm, D // 2), lambda j, f, te, tr, tb: (tb[j], 0)),
                      pl.BlockSpec((sq, sq, D, tf), lambda j, f, te, tr, tb: (layer, te[j], 0, hold(f, nf - 1, tr, j))),
                      pl.BlockSpec((sq, sq, D, tf), lambda j, f, te, tr, tb: (layer, te[j], 0, hold(f, nf - 1, tr, j)))],
            out_specs=pl.BlockSpec((tm, tf), lambda j, f, te, tr, tb: (tb[j], hold(f, nf - 1, tr, j))),
            scratch_shapes=[pltpu.VMEM((tm, D), BF16)],
        ),
        out_shape=jax.ShapeDtypeStruct((nt * tm, F), BF16),
        compiler_params=_params(("arbitrary", "arbitrary")),
        name="moe_up",
    )(tile_expert, tile_rows, tile_block, xg, w_gate, w_up)
    return pl.pallas_call(
        _moe_down_kernel,
        grid_spec=pltpu.PrefetchScalarGridSpec(
            num_scalar_prefetch=3,
            grid=(nt, nc),
            in_specs=[pl.BlockSpec((tm, F), lambda j, c, te, tr, tb: (tb[j], 0)),
                      pl.BlockSpec((sq, sq, F, tn), lambda j, c, te, tr, tb: (layer, te[j], 0, hold(c, nc - 1, tr, j)))],
            out_specs=pl.BlockSpec((tm, tn), lambda j, c, te, tr, tb: (tb[j], hold(c, nc - 1, tr, j))),
        ),
        out_shape=jax.ShapeDtypeStruct((nt * tm, D), F32),
        compiler_params=_params(("arbitrary", "arbitrary")),
        name="moe_down",
    )(tile_expert, tile_rows, tile_block, act, w_down)


def _combine_kernel(slot_ref, y_ref, gate_ref, o_ref, ya_ref, yb_ref, sem):
    n = slot_ref.shape[0] // 2
    base = pl.program_id(0) * COPY_ROWS

    def copy(r, k):
        dst = ya_ref if k == 0 else yb_ref
        return pltpu.make_async_copy(y_ref.at[pl.ds(slot_ref[k * n + base + r], 1)], dst.at[pl.ds(r, 1)], sem)

    def start(r, carry):
        copy(r, 0).start()
        copy(r, 1).start()
        return carry

    def wait(r, carry):
        copy(r, 0).wait()
        copy(r, 1).wait()
        return carry

    lax.fori_loop(0, COPY_ROWS, start, 0)
    lax.fori_loop(0, COPY_ROWS, wait, 0)
    g = gate_ref[...].T
    o_ref[...] = ya_ref[...] * g[:, 0:1] + yb_ref[...] * g[:, 1:2]


def moe_combine(slots, y, gates):
    n = gates.shape[1]
    D = y.shape[1]
    return pl.pallas_call(
        _combine_kernel,
        grid_spec=pltpu.PrefetchScalarGridSpec(
            num_scalar_prefetch=1,
            grid=(n // COPY_ROWS,),
            in_specs=[pl.BlockSpec(memory_space=pl.ANY),
                      pl.BlockSpec((V7X_SUBLANES, COPY_ROWS), lambda i, s: (0, i))],
            out_specs=pl.BlockSpec((COPY_ROWS, D), lambda i, s: (i, 0)),
            scratch_shapes=[pltpu.VMEM((COPY_ROWS, D), F32), pltpu.VMEM((COPY_ROWS, D), F32),
                            pltpu.SemaphoreType.DMA(())],
        ),
        out_shape=jax.ShapeDtypeStruct((n, D), F32),
        compiler_params=_params(("arbitrary",)),
        name="moe_combine",
    )(slots, y, gates)


def _na_kernel(rpb_ref, q_ref, k_ref, v_ref, kc_ref, vc_ref, o_ref, tb_ref):
    h = pl.program_id(1)
    GW, HD = GRID_W, V7X_LANES
    n_rows = q_ref.shape[1] // GW
    wr = NA_WIN_R
    n_dc = 2 * NA_WIN_C - 1
    qcol = lax.broadcasted_iota(jnp.int32, (GW, 2 * GW), 0)
    lane = lax.broadcasted_iota(jnp.int32, (GW, 2 * GW), 1)
    kcol = lane & (GW - 1)
    upper = lane >= GW
    dci = jnp.clip(kcol - qcol + NA_WIN_C - 1, 0, n_dc - 1)
    cstart = jnp.clip(qcol - NA_WIN_C // 2, 0, GW - NA_WIN_C)
    col_ok = (kcol >= cstart) & (kcol < cstart + NA_WIN_C)
    for dr in range(2 * wr - 2):
        acc = jnp.zeros((GW, 2 * GW), F32)
        for j in range(n_dc):
            lo = rpb_ref[h, dr * n_dc + j]
            hi = rpb_ref[h, (dr + 1) * n_dc + j]
            acc = jnp.where(dci == j, jnp.where(upper, hi, lo), acc)
        tb_ref[dr] = acc
    ok4 = jnp.concatenate([col_ok] * (wr // 2), axis=1)
    kc = kc_ref[0]
    vc = vc_ref[0]
    scale = HD ** -0.5

    def body(r, carry):
        rs = jnp.clip(r - wr // 2, 0, n_rows - wr)
        q = q_ref[0, pl.ds(pl.multiple_of(r * GW, GW), GW), :] * scale
        k0 = pl.multiple_of(rs * GW, GW)
        kw = k_ref[0, pl.ds(k0, wr * GW), :]
        vw = v_ref[0, pl.ds(k0, wr * GW), :]
        dr0 = rs - r + wr - 1
        bias = jnp.concatenate([tb_ref[dr0 + 2 * w] for w in range(wr // 2)], axis=1)
        s_lat = jnp.where(ok4, _bdot_nt(q, kw) + bias, NEG_INF)
        s_ctx = _bdot_nt(q, kc)
        m = jnp.maximum(jnp.max(s_lat, axis=-1, keepdims=True), jnp.max(s_ctx, axis=-1, keepdims=True))
        p_lat = jnp.exp(s_lat - m)
        p_ctx = jnp.exp(s_ctx - m)
        den = jnp.sum(p_lat, axis=-1, keepdims=True) + jnp.sum(p_ctx, axis=-1, keepdims=True)
        o = (_bdot(p_lat, vw) + _bdot(p_ctx, vc)) / den
        o_ref[0, pl.ds(pl.multiple_of(r * GW, GW), GW), :] = o
        return carry

    lax.fori_loop(0, n_rows, body, 0)


def neighbourhood_attention(pl_lat, kv_ctx, rpb):
    B, S, _ = pl_lat.shape
    Lc = kv_ctx.shape[1]
    H, HD = NA_HEADS, V7X_LANES
    lat = lambda off: pl.BlockSpec((1, S, HD), lambda b, h: (b, 0, off + h))
    cx = lambda off: pl.BlockSpec((1, Lc, HD), lambda b, h: (b, 0, off + h))
    return pl.pallas_call(
        _na_kernel,
        grid=(B, H),
        in_specs=[pl.BlockSpec(memory_space=pltpu.SMEM), lat(0), lat(H), lat(2 * H), cx(0), cx(H)],
        out_specs=pl.BlockSpec((1, S, HD), lambda b, h: (b, 0, h)),
        out_shape=jax.ShapeDtypeStruct((B, S, H * HD), F32),
        scratch_shapes=[pltpu.VMEM((2 * NA_WIN_R - 2, GRID_W, 2 * GRID_W), F32)],
        compiler_params=_params(("parallel", "parallel")),
        name="neighbourhood_attention",
    )(rpb.reshape(H, -1), pl_lat, pl_lat, pl_lat, kv_ctx, kv_ctx)


DFT_N1 = 64
DFT_N2 = 128
DFT_SUB = 8


@functools.lru_cache(maxsize=None)
def _dft_constants():
    n1, n2 = DFT_N1, DFT_N2
    n = n1 * n2
    a = np.arange(n1)
    f1 = np.exp(-2j * np.pi * np.outer(a, a) / n1)
    half = f1[:, :n1 // 2]
    eye = np.eye(DFT_SUB)
    fwd = np.kron(half, eye)
    inv = np.kron(np.conj(f1)[:n1 // 2, :], eye) / n
    b = np.arange(n2)
    f2 = np.exp(-2j * np.pi * np.outer(b, b) / n2)
    tw = np.exp(-2j * np.pi * np.outer(a, b) / n)
    m = f2[None, :, :] * tw[:, None, :]
    mt = np.transpose(m, (0, 2, 1))
    c = lambda z: (np.ascontiguousarray(z.real, dtype=np.float32), np.ascontiguousarray(z.imag, dtype=np.float32))
    return c(fwd) + c(inv) + c(m) + c(mt)


def _hyena_conv_kernel(hf_ref, hb_ref, z_ref, fwr_ref, fwi_ref, ivr_ref, ivi_ref,
                       mr_ref, mi_ref, mtr_ref, mti_ref, o_ref, ar_ref, ai_ref, kr_ref, ki_ref):
    L, TC = hf_ref.shape
    N1, N2, SUB = DFT_N1, DFT_N2, DFT_SUB
    nblk = N2 // SUB

    def stage1(load):
        def blk(nb, carry):
            off = pl.multiple_of(nb * SUB, SUB)
            xs = jnp.concatenate([load(n1 * N2 + off) for n1 in range(N1 // 2)], axis=0)
            yr = _bdot(fwr_ref[...], xs)
            yi = _bdot(fwi_ref[...], xs)
            for k1 in range(N1):
                ar_ref[pl.ds(k1 * N2 + off, SUB), :] = yr[k1 * SUB:(k1 + 1) * SUB]
                ai_ref[pl.ds(k1 * N2 + off, SUB), :] = yi[k1 * SUB:(k1 + 1) * SUB]
            return carry

        lax.fori_loop(0, nblk, blk, 0)

    def spectrum(k1):
        rows = pl.ds(pl.multiple_of(k1 * N2, N2), N2)
        a_r, a_i = ar_ref[rows, :], ai_ref[rows, :]
        m_r, m_i = mr_ref[k1], mi_ref[k1]
        return rows, _bdot(m_r, a_r) - _bdot(m_i, a_i), _bdot(m_r, a_i) + _bdot(m_i, a_r)

    stage1(lambda r0: hf_ref[pl.ds(r0, SUB), :])

    def filt_f(k1, carry):
        rows, x_r, x_i = spectrum(k1)
        kr_ref[rows, :] = x_r
        ki_ref[rows, :] = x_i
        return carry

    lax.fori_loop(0, N1, filt_f, 0)
    sub_iota = lax.broadcasted_iota(jnp.int32, (SUB, TC), 0)
    stage1(lambda r0: jnp.where(sub_iota + r0 == 0, 0.0, hb_ref[pl.ds(r0, SUB), :]))

    def filt_b(k1, carry):
        rows, x_r, x_i = spectrum(k1)
        kr_ref[rows, :] += x_r
        ki_ref[rows, :] -= x_i
        return carry

    lax.fori_loop(0, N1, filt_b, 0)

    for b in range(z_ref.shape[0]):
        stage1(lambda r0: z_ref[b, pl.ds(r0, SUB), :])

        def conv(k1, carry):
            rows, x_r, x_i = spectrum(k1)
            f_r, f_i = kr_ref[rows, :], ki_ref[rows, :]
            y_r = x_r * f_r - x_i * f_i
            y_i = x_r * f_i + x_i * f_r
            t_r, t_i = mtr_ref[k1], mti_ref[k1]
            ar_ref[rows, :] = _bdot(t_r, y_r) + _bdot(t_i, y_i)
            ai_ref[rows, :] = _bdot(t_r, y_i) - _bdot(t_i, y_r)
            return carry

        lax.fori_loop(0, N1, conv, 0)
        def last(nb, carry):
            off = pl.multiple_of(nb * SUB, SUB)
            br = jnp.concatenate([ar_ref[pl.ds(k1 * N2 + off, SUB), :] for k1 in range(N1)], axis=0)
            bi = jnp.concatenate([ai_ref[pl.ds(k1 * N2 + off, SUB), :] for k1 in range(N1)], axis=0)
            y = _bdot(ivr_ref[...], br) - _bdot(ivi_ref[...], bi)
            for n1 in range(N1 // 2):
                o_ref[b, pl.ds(n1 * N2 + off, SUB), :] = y[n1 * SUB:(n1 + 1) * SUB]
            return carry

        lax.fori_loop(0, nblk, last, 0)


def hyena_long_conv(z, h_f, h_b, *, tc=V7X_LANES):
    B, L, C = z.shape
    assert L == DFT_N1 * DFT_N2 // 2
    consts = [jnp.asarray(a, BF16) for a in _dft_constants()]
    whole = lambda a: pl.BlockSpec(a.shape, lambda j: (0,) * a.ndim)
    col = pl.BlockSpec((L, tc), lambda j: (0, j))
    bcol = pl.BlockSpec((B, L, tc), lambda j: (0, 0, j))
    return pl.pallas_call(
        _hyena_conv_kernel,
        grid=(C // tc,),
        in_specs=[col, col, bcol] + [whole(a) for a in consts],
        out_specs=bcol,
        out_shape=jax.ShapeDtypeStruct((B, L, C), F32),
        scratch_shapes=[pltpu.VMEM((2 * L, tc), F32)] * 4,
        compiler_params=_params(("parallel",)),
        name="hyena_long_conv",
    )(h_f, h_b, z, *consts)


def _pick(n, options):
    for t in options:
        if n % t == 0:
            return t
    raise ValueError(f"no tile for {n} in {options}")


def _mm(a, w, **kw):
    tm = _pick(a.shape[0], (1088, 1024, 768, 512, 256, 8))
    tn = _pick(w.shape[1], (512, 640, 384, 256, 128))
    return matmul(a, w, tm=tm, tn=tn, **kw)


def _pad_to(a, axis, mult):
    pad = (-a.shape[axis]) % mult
    if pad == 0:
        return a
    widths = [(0, 0)] * a.ndim
    widths[axis] = (0, pad)
    return jnp.pad(a, widths)


def _silu(x):
    return x * jax.nn.sigmoid(x)


def _seq_parts(x, lc, fn):
    if lc == 0:
        return fn(x)
    return jnp.concatenate([fn(x[:, :lc]), fn(x[:, lc:])], axis=1)


def _dwconv(x, w):
    k, length = w.shape[0], x.shape[1]
    xp = jnp.pad(x, ((0, 0), (k // 2, k // 2), (0, 0)))
    return sum(xp[:, j:j + length] * w[j] for j in range(k))


def _centred_shift(p):
    prev = jnp.pad(p, ((0, 0), (1, 0), (0, 0)))[:, :-1]
    nxt = jnp.pad(p, ((0, 0), (0, 1), (0, 0)))[:, 1:]
    return 0.5 * (prev + nxt)


def _l2norm(t):
    return t * lax.rsqrt(jnp.sum(t * t, -1, keepdims=True) + 1e-6)


def _axial_rope(t):
    length, d = t.shape[1], t.shape[-1]
    nf = d // 4
    pos = jnp.arange(length)
    inv = ROPE_BASE ** (-jnp.arange(nf, dtype=F32) / nf)

    def cs(p):
        ang = p.astype(F32)[:, None] * inv[None, :]
        return jnp.cos(ang), jnp.sin(ang)

    cr, sr = cs(pos // GRID_W)
    cc, sc = cs(pos % GRID_W)
    cos = jnp.concatenate([cr, cr, cc, cc], -1)[None, :, None, :]
    sin = jnp.concatenate([-sr, sr, -sc, sc], -1)[None, :, None, :]
    swapped = jnp.flip(t.reshape(*t.shape[:-1], 2, 2, nf), axis=-2).reshape(t.shape)
    return t * cos + swapped * sin


def _mixer_ab(h, lc, w_in, dn_conv_w, dn_a_log, dn_dt_bias, dn_norm_w,
              rw_mu, rw_w0, rw_w2, rw_a0, rw_a2, rw_g2, rw_k_k, rw_k_a, rw_r_k, rw_ln_w, rw_ln_b, w_out):
    B, T, D = h.shape
    mw = D // 2
    hd_dn = mw // DN_HEADS
    h_rw = mw // RW_HD
    p_a = 4 * mw + 4 * DN_HEADS
    h2 = h.reshape(B * T, D)
    pa = _mm(h2, _pad_to(w_in[:, :p_a], 1, V7X_LANES).astype(BF16)).reshape(B, T, -1)
    pb = _mm(h2, _pad_to(w_in[:, p_a:], 1, V7X_LANES).astype(BF16)).reshape(B, T, -1)

    qkv = _seq_parts(pa[..., :3 * mw], lc, lambda u: _silu(_dwconv(u, dn_conv_w)))
    q, k, v = (t.reshape(B, T, DN_HEADS, hd_dn) for t in jnp.split(qkv, 3, axis=-1))
    rope_lat = lambda t: jnp.concatenate([t[:, :lc], _axial_rope(t[:, lc:])], axis=1)
    q, k = rope_lat(_l2norm(q)), rope_lat(_l2norm(k))
    q = q * hd_dn ** -0.5
    z = pa[..., 3 * mw:4 * mw]
    bg = pa[..., 4 * mw:p_a].reshape(B, T, 4, DN_HEADS)
    beta = jax.nn.sigmoid(bg[:, :, :2])
    g = -jnp.exp(dn_a_log) * jax.nn.softplus(bg[:, :, 2:] + dn_dt_bias)
    flat = lambda t: t.reshape(B, T, mw)
    o = 0.0
    for d in range(2):
        bk = k * beta[:, :, d, :, None]
        gd = jnp.broadcast_to(g[:, :, d, :, None], k.shape)
        o = o + delta_scan(flat(q), flat(k), None, flat(bk), flat(v), flat(gd),
                           head_dim=hd_dn, reverse=d == 1, scalar_decay=True, beta_excl=False)
    o = o.reshape(B, T, DN_HEADS, hd_dn)
    o = o * lax.rsqrt(jnp.mean(o * o, -1, keepdims=True) + NORM_EPS) * dn_norm_w
    dn = flat(o * _silu(z).reshape(o.shape))

    p_b = w_in.shape[1] - p_a
    p = pb[..., :p_b]
    p = p + (_seq_parts(p, lc, _centred_shift) - p) * rw_mu
    r, k, v = p[..., :mw], p[..., mw:2 * mw], p[..., 2 * mw:3 * mw]
    o1 = 3 * mw
    n_w, n_a = rw_w2.shape[1], rw_a2.shape[1]
    o2 = o1 + 2 * n_w
    o3 = o2 + 2 * n_a
    wl = p[..., o1:o2].reshape(B * T, 2, n_w)
    al = p[..., o2:o3].reshape(B * T, 2, n_a)
    gl = p[..., o3:].reshape(B * T, -1)
    gate = _mm(_pad_to(jax.nn.sigmoid(gl), 1, V7X_LANES), _pad_to(rw_g2, 0, V7X_LANES)).reshape(B, T, mw)
    heads = lambda t: t.reshape(B, T, h_rw, RW_HD)
    kk = flat(_l2norm(heads(k * rw_k_k)))
    y = 0.0
    for d in range(2):
        w = -jax.nn.softplus(-(rw_w0[d] + _mm(jnp.tanh(wl[:, d]), rw_w2[d]).reshape(B, T, mw))) - 0.5
        a = jax.nn.sigmoid(rw_a0[d] + _mm(al[:, d], rw_a2[d]).reshape(B, T, mw))
        kd = k * (1.0 + (a - 1.0) * rw_k_a)
        y = y + delta_scan(r, kk, -(kk * a), kd, v, -jnp.exp(w),
                           head_dim=RW_HD, reverse=d == 1, scalar_decay=False, beta_excl=True)
    y = heads(y)
    mu = jnp.mean(y, -1, keepdims=True)
    var = jnp.mean(jnp.square(y - mu), -1, keepdims=True)
    yn = flat((y - mu) * lax.rsqrt(var + RW_LN_EPS)) * rw_ln_w + rw_ln_b
    bonus = flat(jnp.sum(heads(r) * heads(k) * rw_r_k, -1, keepdims=True) * heads(v))
    rw = (yn + bonus) * gate

    cat = jnp.concatenate([dn, rw], -1).astype(BF16).reshape(B * T, D)
    return _mm(cat, w_out.astype(BF16)).reshape(B, T, D)


def _hyena_filters(length, ch, w1, b1, freq, w2, b2, w3, b3, w4):
    t = jnp.linspace(0.0, 1.0, length, dtype=F32)
    bands = (HY_EMB - 1) // 2
    wpos = 2 * math.pi * jnp.arange(length, dtype=F32) / length
    fb = jnp.linspace(1e-4, bands - 1, bands, dtype=F32)
    ang = wpos[:, None] * fb[None, :]
    z = jnp.concatenate([t[:, None], jnp.cos(ang), -jnp.sin(ang)], axis=-1)
    dense = lambda u, w: _mm(_pad_to(u, 1, V7X_LANES), _pad_to(_pad_to(w, 0, V7X_LANES), 1, V7X_LANES),
                             passes=3)[:, :w.shape[1]]
    h = jnp.sin(freq * (dense(z, w1) + b1))
    h = jnp.sin(freq * (dense(h, w2) + b2))
    h = jnp.sin(freq * (dense(h, w3) + b3))
    h = dense(h, w4).reshape(length, 2, ch)
    deltas = jnp.abs(jnp.linspace(HY_MIN_DECAY, HY_MAX_DECAY, ch, dtype=F32))
    window = jnp.exp(-t[:, None] * deltas[None, :])
    return h[:, 0] * window, h[:, 1] * window


def _mixer_cd(hc, hl, w_in, na_rpb, hy_conv_w, hy_conv_b, filt, hy_skip, w_out):
    B, S, D = hl.shape
    lc = hc.shape[1]
    mw = D // 2
    w_bf = w_in.astype(BF16)
    pl_ = _mm(hl.reshape(B * S, D), w_bf).reshape(B, S, -1)
    kv_c = _mm(hc.reshape(B * lc, D), w_bf[:, mw:3 * mw]).reshape(B, lc, 2 * mw)
    att = neighbourhood_attention(pl_, kv_c, na_rpb)
    u = _dwconv(pl_[..., 3 * mw:], hy_conv_w) + hy_conv_b
    x0, x1, v = jnp.split(u, 3, axis=-1)
    h_f, h_b = _hyena_filters(S, mw, *filt)
    z = x1 * v
    hy = x0 * (hyena_long_conv(z, h_f, h_b) + z * hy_skip)
    cat = jnp.concatenate([att, hy], -1).astype(BF16).reshape(B * S, D)
    return _mm(cat, w_out.astype(BF16)).reshape(B, S, D)


def _moe(t_packed, t_f32, router_w, router_bias, w_gate, w_up, w_down, layer, tile=MOE_TILE):
    n, D = t_f32.shape
    E = w_gate.shape[1]
    idx, gates, counts = moe_route(t_f32, router_w, router_bias)
    cnt = counts[:, 0].astype(jnp.int32)
    nt = -(-2 * n // tile) + E
    tiles = (cnt + tile - 1) // tile
    ends = jnp.cumsum(tiles)
    starts = ends - tiles
    used = ends[-1]
    j = jnp.arange(nt, dtype=jnp.int32)
    jj = jnp.minimum(j, used - 1)
    tile_expert = jnp.sum((jj[:, None] >= ends[None, :]).astype(jnp.int32), axis=1)
    tile_rows = jnp.where(j < used, jnp.clip(cnt[tile_expert] - (jj - starts[tile_expert]) * tile, 0, tile), 0)
    row0 = starts * tile
    slots = jnp.concatenate([row0[idx[0]] + idx[2], row0[idx[1]] + idx[3]]).astype(jnp.int32)
    xg = moe_dispatch(slots, t_packed, nt * tile)
    y = moe_experts(xg, w_gate, w_up, w_down, tile_expert.astype(jnp.int32), tile_rows.astype(jnp.int32), jj,
                    layer=layer, tm=tile)
    return moe_combine(slots, y, gates)


def kernel(x, c, ctx, c_ctx, ada_w, ada_b, norm1_g, norm2_g, final_g, ab_w_in, dn_conv_w, dn_a_log, dn_dt_bias, dn_norm_w, rw_mu, rw_w0, rw_w2, rw_a0, rw_a2, rw_g2, rw_k_k, rw_k_a, rw_r_k, rw_ln_w, rw_ln_b, ab_w_out, cd_w_in, na_rpb, hy_conv_w, hy_conv_b, hy_w1, hy_b1, hy_freq, hy_w2, hy_b2, hy_w3, hy_b3, hy_w4, hy_skip, cd_w_out, router_w, router_bias, moe_w_gate, moe_w_up, moe_w_down):
    B, S, D = x.shape
    lc = ctx.shape[1]
    assert ada_w.shape[0] == 2 and lc == ROW_TILE and S % ROW_TILE == 0 and B + 1 <= V7X_SUBLANES
    T = lc + S
    ctx_tiles = lc // ROW_TILE
    cond = jnp.concatenate([c, c_ctx[None], jnp.zeros((V7X_SUBLANES - B - 1, D), F32)], axis=0)
    mods = [matmul(cond, ada_w, tm=V7X_SUBLANES, tn=512, bias=ada_b[l], a_silu=True, layer=l,
                   name="adaln")[:B + 1].reshape(B + 1, 6, D) for l in range(2)]

    def rows(m, idx):
        return jnp.concatenate([jnp.broadcast_to(m[B, idx], (B, lc, D)),
                                jnp.broadcast_to(m[:B, idx][:, None, :], (B, S, D))], axis=1)

    m = mods[0]
    xa = jnp.concatenate([ctx, x], axis=1)
    h = norm_mod(xa, norm1_g[0], m, which=0, ctx_tiles=ctx_tiles)
    mix = _mixer_ab(h, lc, ab_w_in[0], dn_conv_w[0], dn_a_log[0], dn_dt_bias[0], dn_norm_w[0],
                    rw_mu[0], rw_w0[0], rw_w2[0], rw_a0[0], rw_a2[0], rw_g2[0], rw_k_k[0], rw_k_a[0],
                    rw_r_k[0], rw_ln_w[0], rw_ln_b[0], ab_w_out[0])
    xa = xa + rows(m, 2) * mix
    h_pk, h_f32 = norm_mod(xa, norm2_g[0], m, which=1, ctx_tiles=ctx_tiles, kinds=("packed", "f32"))
    f = _moe(h_pk.reshape(B * T, D // 2), h_f32.reshape(B * T, D), router_w, router_bias,
             moe_w_gate, moe_w_up, moe_w_down, 0).reshape(B, T, D)
    xa = xa + rows(m, 5) * f

    m = mods[1]
    cx, xl = xa[:, :lc], xa[:, lc:]
    hl = norm_mod(xl, norm1_g[1], m, which=0, ctx_tiles=0)
    hc = norm_mod(cx, norm1_g[1], m, which=0, ctx_tiles=ctx_tiles)
    filt = (hy_w1[0], hy_b1[0], hy_freq[0], hy_w2[0], hy_b2[0], hy_w3[0], hy_b3[0], hy_w4[0])
    ml = _mixer_cd(hc, hl, cd_w_in[0], na_rpb[0], hy_conv_w[0], hy_conv_b[0], filt, hy_skip[0], cd_w_out[0])
    xl = xl + m[:B, 2][:, None, :] * ml
    h_pk, h_f32 = norm_mod(xl, norm2_g[1], m, which=1, ctx_tiles=0, kinds=("packed", "f32"))
    f = _moe(h_pk.reshape(B * S, D // 2), h_f32.reshape(B * S, D), router_w, router_bias,
             moe_w_gate, moe_w_up, moe_w_down, 1).reshape(B, S, D)
    xl = xl + m[:B, 5][:, None, :] * f
    return xl * lax.rsqrt(jnp.mean(xl * xl, -1, keepdims=True) + NORM_EPS) * final_g
```

```python
import functools
import math

import jax
import jax.numpy as jnp
import numpy as np
from jax import lax
from jax.experimental import pallas as pl
from jax.experimental.pallas import tpu as pltpu

F32 = jnp.float32
BF16 = jnp.bfloat16

V7X_LANES = 128
V7X_SUBLANES = 8
V7X_MXU_DIM = 256
V7X_VMEM_LIMIT_BYTES = 56 * 1024 * 1024

GRID_W = 64
NORM_EPS = 1e-6
NEG_INF = -1e30
ROPE_BASE = 10000.0
DN_HEADS = 16
DN_CONV = 5
RW_HD = 64
RW_LN_EPS = 64e-5
NA_HEADS = 16
NA_WIN_R = 8
NA_WIN_C = 16
HY_EMB = 33
HY_TARGET = 1e-2
HY_MAX_DECAY = math.log(HY_TARGET) / 0.3
HY_MIN_DECAY = math.log(HY_TARGET) / 1.5
N_EXPERTS = 16
N_GROUPS = 4

ROW_TILE = 256
SCAN_CHUNK = 64


def _params(semantics, vmem=V7X_VMEM_LIMIT_BYTES):
    return pltpu.CompilerParams(dimension_semantics=semantics, vmem_limit_bytes=vmem)


def _bdot(a, b):
    return jnp.dot(a.astype(BF16), b.astype(BF16), preferred_element_type=F32)


def _bdot_nt(a, b):
    return lax.dot_general(a.astype(BF16), b.astype(BF16), (((1,), (1,)), ((), ())),
                           preferred_element_type=F32)


def _split2(x):
    hi = x.astype(BF16)
    lo = (x - hi.astype(F32)).astype(BF16)
    return hi, lo


def _split3(x):
    hi = x.astype(BF16)
    r1 = x - hi.astype(F32)
    mid = r1.astype(BF16)
    lo = (r1 - mid.astype(F32)).astype(BF16)
    return hi, mid, lo


def _dot3(a, b):
    ah, al = _split2(a)
    bh, bl = _split2(b)
    d = lambda u, v: jnp.dot(u, v, preferred_element_type=F32)
    return d(ah, bh) + (d(ah, bl) + d(al, bh))


def _dot_exact_left(m, x):
    hi, mid, lo = _split3(x)
    d = lambda v: jnp.dot(m, v, preferred_element_type=F32)
    return d(hi) + (d(mid) + d(lo))


def _unit_tri_inverse(mats, row, col):
    eye = (row == col).astype(F32)
    in8 = (row >> 3) == (col >> 3)
    a8 = [jnp.where(in8, a, 0.0) for a in mats]
    a2 = [_bdot(u, u) for u in a8]
    p = [eye + u for u in a8]
    p = [u + _bdot(u, w) for u, w in zip(p, a2)]
    a4 = [_bdot(u, u) for u in a2]
    x = [u + _bdot(u, w) for u, w in zip(p, a4)]
    for sh in (3, 4, 5):
        m = ((row >> (sh + 1)) == (col >> (sh + 1))) & ((row >> sh) != (col >> sh))
        left = [_bdot(u, jnp.where(m, a, 0.0)) for u, a in zip(x, mats)]
        x = [u + _bdot(w, u) for u, w in zip(x, left)]
    return x


def _delta_scan_kernel(*refs, lane_groups, head_dim, reverse, scalar_decay, beta_excl, ka_is_neg_kk):
    if ka_is_neg_kk:
        r_ref, kb_ref, kk_ref, v_ref, g_ref, y_ref, t_ref = refs
        ka_ref = None
    else:
        r_ref, kb_ref, ka_ref, kk_ref, v_ref, g_ref, y_ref, t_ref = refs
    TT, W, C = ROW_TILE, V7X_MXU_DIM, SCAN_CHUNK
    n_heads = W // head_dim
    hshift = int(math.log2(head_dim))
    groups = range(lane_groups)
    units = [(gs, j) for gs in groups for j in range(n_heads)]

    @pl.when(pl.program_id(2) == 0)
    def _():
        t_ref[...] = jnp.zeros_like(t_ref)

    row = lax.broadcasted_iota(jnp.int32, (TT, TT), 0)
    col = lax.broadcasted_iota(jnp.int32, (TT, TT), 1)
    same = (row >> 6) == (col >> 6)
    if reverse:
        strict, incl = same & (col > row), same & (col >= row)
    else:
        strict, incl = same & (col < row), same & (col <= row)
    incl_bf, same_bf = incl.astype(BF16), same.astype(BF16)
    load = lambda ref: [ref[0, :, gs * W:(gs + 1) * W] for gs in groups]

    g = load(g_ref)
    gi = [_dot_exact_left(incl_bf, u) for u in g]
    gc = [_dot_exact_left(same_bf, u) for u in g]
    gb = [u - w for u, w in zip(gi, g)] if beta_excl else gi
    r, kb, kk, v = load(r_ref), load(kb_ref), load(kk_ref), load(v_ref)
    ka = [-u for u in kk] if ka_is_neg_kk else load(ka_ref)
    e_rest = [jnp.exp(c - i) for c, i in zip(gc, gi)]
    r_s = [u * jnp.exp(i) for u, i in zip(r, gi)]
    kb_s = [u * jnp.exp(i) for u, i in zip(kb, gb)]
    ka_e = [u * e for u, e in zip(ka, e_rest)]
    kk_e = [u * e for u, e in zip(kk, e_rest)]
    if scalar_decay:
        assert head_dim == V7X_LANES
        gram_r, gram_b, gram_k = r, kb, kk
        gram_a = None if ka_is_neg_kk else ka
    else:
        inv = [jnp.exp(-i) for i in gi]
        gram_r, gram_b = r_s, kb_s
        gram_k = [u * w for u, w in zip(kk, inv)]
        gram_a = None if ka_is_neg_kk else [u * w for u, w in zip(ka, inv)]

    lane_head = lax.broadcasted_iota(jnp.int32, (TT, W), 1) >> hshift
    mj = [lane_head == j for j in range(n_heads)]
    bl = [jnp.where(mj[j], gram_b[gs], 0.0) for gs, j in units]
    rl = [jnp.where(mj[j], gram_r[gs], 0.0) for gs, j in units]
    a_bk = [_bdot_nt(bl[u], gram_k[gs]) for u, (gs, j) in enumerate(units)]
    a_rk = [_bdot_nt(rl[u], gram_k[gs]) for u, (gs, j) in enumerate(units)]
    if ka_is_neg_kk:
        a_ba, a_ra = [-u for u in a_bk], [-u for u in a_rk]
    else:
        a_ba = [_bdot_nt(bl[u], gram_a[gs]) for u, (gs, j) in enumerate(units)]
        a_ra = [_bdot_nt(rl[u], gram_a[gs]) for u, (gs, j) in enumerate(units)]
    if scalar_decay:
        for u, (gs, j) in enumerate(units):
            sl = slice(j * head_dim, (j + 1) * head_dim)
            gi_m = jnp.concatenate([gi[gs][:, sl]] * n_heads, axis=1)
            gb_m = jnp.concatenate([gb[gs][:, sl]] * n_heads, axis=1)
            gi_t = gi_m.T
            d_b = jnp.exp(jnp.minimum(gb_m - gi_t, 0.0))
            d_r = jnp.exp(jnp.minimum(gi_m - gi_t, 0.0))
            a_ba[u], a_bk[u], a_ra[u], a_rk[u] = a_ba[u] * d_b, a_bk[u] * d_b, a_ra[u] * d_r, a_rk[u] * d_r
    a_ba = [jnp.where(strict, u, 0.0) for u in a_ba]
    a_bk = [jnp.where(strict, u, 0.0) for u in a_bk]
    a_ra = [jnp.where(incl, u, 0.0) for u in a_ra]
    a_rk = [jnp.where(incl, u, 0.0) for u in a_rk]
    x = _unit_tri_inverse(a_ba, row, col)
    bj = [_bdot(x[u], kb_s[gs]) for u, (gs, j) in enumerate(units)]
    t1 = [_bdot(a_bk[u], v[gs]) for u, (gs, j) in enumerate(units)]
    wj = [_bdot(x[u], t1[u]) for u in range(len(units))]
    ra = [_bdot(a_ra[u], bj[u]) for u in range(len(units))]
    ya = [_bdot(a_ra[u], wj[u]) for u in range(len(units))]
    yb = [_bdot(a_rk[u], v[gs]) for u, (gs, j) in enumerate(units)]
    b_hat, w1, r_hat, y0 = [], [], [], []
    for gs in groups:
        bh = wh = yh = jnp.zeros((TT, W), F32)
        rh = r_s[gs]
        for j in range(n_heads):
            u = gs * n_heads + j
            bh = jnp.where(mj[j], bj[u], bh)
            wh = jnp.where(mj[j], wj[u], wh)
            rh = rh + jnp.where(mj[j], ra[u], 0.0)
            yh = jnp.where(mj[j], ya[u] + yb[u], yh)
        b_hat.append(bh)
        w1.append(wh)
        r_hat.append(rh)
        y0.append(yh)

    rw = lax.broadcasted_iota(jnp.int32, (W, W), 0)
    cw = lax.broadcasted_iota(jnp.int32, (W, W), 1)
    blockdiag = (rw >> hshift) == (cw >> hshift)
    eye_w = rw == cw
    t = [t_ref[gs] for gs in groups]
    for c in (range(TT // C - 1, -1, -1) if reverse else range(TT // C)):
        sl = slice(c * C, (c + 1) * C)
        for gs in groups:
            y_ref[0, sl, gs * W:(gs + 1) * W] = _bdot(r_hat[gs][sl], t[gs]) + y0[gs][sl]
        ka_t = [ka_e[gs][sl].T for gs in groups]
        kk_t = [kk_e[gs][sl].T for gs in groups]
        decay_col = [jnp.sum(jnp.where(eye_w, jnp.exp(gc[gs][c * C:c * C + 1, :]), 0.0), axis=1, keepdims=True)
                     for gs in groups]
        mix = [jnp.where(blockdiag, _bdot(ka_t[gs], b_hat[gs][sl]), 0.0) for gs in groups]
        add = [jnp.where(blockdiag, _bdot(ka_t[gs], w1[gs][sl]) + _bdot(kk_t[gs], v[gs][sl]), 0.0) for gs in groups]
        t = [decay_col[gs] * t[gs] + _bdot(mix[gs], t[gs]) + add[gs] for gs in groups]
    for gs in groups:
        t_ref[gs] = t[gs]


def delta_scan(r, kb, ka, kk, v, g, *, head_dim, reverse, scalar_decay, beta_excl, lane_groups=2):
    B, T, HW = r.shape
    n_t = T // ROW_TILE
    width = lane_groups * V7X_MXU_DIM
    if reverse:
        tile = lambda i: jnp.where(i == 0, 0, n_t - i)
    else:
        tile = lambda i: i
    spec = pl.BlockSpec((1, ROW_TILE, width), lambda b, h, i: (b, tile(i), h))
    kern = functools.partial(_delta_scan_kernel, lane_groups=lane_groups, head_dim=head_dim, reverse=reverse,
                             scalar_decay=scalar_decay, beta_excl=beta_excl, ka_is_neg_kk=ka is None)
    args = [a for a in (r, kb, ka, kk, v, g) if a is not None]
    return pl.pallas_call(
        kern,
        grid=(B, HW // width, n_t),
        in_specs=[spec] * len(args),
        out_specs=spec,
        out_shape=jax.ShapeDtypeStruct((B, T, HW), F32),
        scratch_shapes=[pltpu.VMEM((lane_groups, V7X_MXU_DIM, V7X_MXU_DIM), F32)],
        compiler_params=_params(("parallel", "parallel", "arbitrary")),
        name="delta_scan_rev" if reverse else "delta_scan_fwd",
    )(*args)


def _matmul_kernel(*refs, nk, passes, has_bias, a_silu):
    a_ref, b_ref = refs[0], refs[1]
    bias_ref = refs[2] if has_bias else None
    o_ref = refs[2 + has_bias]
    acc_ref = refs[3 + has_bias] if nk > 1 else None
    a = a_ref[...]
    if a_silu:
        a = a.astype(F32)
        a = a * jax.nn.sigmoid(a)
    if passes == 3:
        part = _dot3(a.astype(F32), b_ref[...].astype(F32))
    else:
        part = _bdot(a, b_ref[...])

    def finish(acc):
        if has_bias:
            acc = acc + bias_ref[...]
        o_ref[...] = acc.astype(o_ref.dtype)

    if nk == 1:
        finish(part)
    else:
        k = pl.program_id(2)

        @pl.when(k == 0)
        def _():
            acc_ref[...] = part

        @pl.when(k > 0)
        def _():
            acc_ref[...] += part

        @pl.when(k == nk - 1)
        def _():
            finish(acc_ref[...])


def matmul(a, b, *, tm, tn, tk=None, bias=None, out_dtype=F32, passes=1, a_silu=False, layer=None,
           name="matmul"):
    M, K = a.shape
    N = b.shape[-1]
    tk = K if tk is None else tk
    assert M % tm == 0 and N % tn == 0 and K % tk == 0, (a.shape, b.shape, tm, tn, tk)
    nk = K // tk
    if layer is None:
        b_spec = pl.BlockSpec((tk, tn), lambda i, j, k: (k, j))
    else:
        b_spec = pl.BlockSpec((pl.Squeezed(), tk, tn), lambda i, j, k: (layer, k, j))
    in_specs = [pl.BlockSpec((tm, tk), lambda i, j, k: (i, k)), b_spec]
    args = [a, b]
    if bias is not None:
        in_specs.append(pl.BlockSpec((1, tn), lambda i, j, k: (0, j)))
        args.append(bias.reshape(1, N).astype(F32))
    kern = functools.partial(_matmul_kernel, nk=nk, passes=passes, has_bias=bias is not None, a_silu=a_silu)
    return pl.pallas_call(
        kern,
        grid=(M // tm, N // tn, nk),
        in_specs=in_specs,
        out_specs=pl.BlockSpec((tm, tn), lambda i, j, k: (i, j)),
        out_shape=jax.ShapeDtypeStruct((M, N), out_dtype),
        scratch_shapes=[pltpu.VMEM((tm, tn), F32)] if nk > 1 else [],
        compiler_params=_params(("parallel", "parallel", "arbitrary")),
        name=name,
    )(*args)


def _pack_halves(h):
    half = h.shape[1] // 2
    bits = lax.bitcast_convert_type(h.astype(BF16).astype(F32), jnp.uint32)
    return (bits[:, half:] & jnp.uint32(0xFFFF0000)) | (bits[:, :half] >> 16)


def _unpack_halves(w):
    lo = lax.bitcast_convert_type(w << 16, F32).astype(BF16)
    hi = lax.bitcast_convert_type(w & jnp.uint32(0xFFFF0000), F32).astype(BF16)
    return lo, hi


def _norm_mod_kernel(x_ref, g_ref, mod_ref, *o_refs, which, kinds):
    x = x_ref[0]
    y = x * lax.rsqrt(jnp.mean(x * x, axis=-1, keepdims=True) + NORM_EPS) * g_ref[...]
    shift = mod_ref[0, 3 * which:3 * which + 1, :]
    scale = mod_ref[0, 3 * which + 1:3 * which + 2, :]
    h = y * (1.0 + scale) + shift
    for kind, o_ref in zip(kinds, o_refs):
        o_ref[0] = _pack_halves(h) if kind == "packed" else h.astype(o_ref.dtype)


def norm_mod(x, g, mod, *, which, ctx_tiles, kinds=("bf16",)):
    B, T, D = x.shape
    ctx_row = mod.shape[0] - 1
    sel = lambda b, j: jnp.where(j < ctx_tiles, ctx_row, b)
    tile = lambda w: pl.BlockSpec((1, ROW_TILE, w), lambda b, j: (b, j, 0))
    shapes = {"bf16": (D, BF16), "f32": (D, F32), "packed": (D // 2, jnp.uint32)}
    outs = pl.pallas_call(
        functools.partial(_norm_mod_kernel, which=which, kinds=kinds),
        grid=(B, T // ROW_TILE),
        in_specs=[tile(D),
                  pl.BlockSpec((1, D), lambda b, j: (0, 0)),
                  pl.BlockSpec((1, 6, D), lambda b, j: (sel(b, j), 0, 0))],
        out_specs=[tile(shapes[k][0]) for k in kinds],
        out_shape=[jax.ShapeDtypeStruct((B, T, shapes[k][0]), shapes[k][1]) for k in kinds],
        compiler_params=_params(("parallel", "parallel")),
        name="norm_mod",
    )(x, g.reshape(1, D), mod)
    return outs[0] if len(kinds) == 1 else outs


def _route_kernel(t_ref, wt_ref, bias_ref, idx_ref, gate_ref, cnt_ref, run_ref):
    E = wt_ref.shape[0]
    tm = t_ref.shape[0]
    per = E // N_GROUPS

    @pl.when(pl.program_id(0) == 0)
    def _():
        run_ref[...] = jnp.zeros_like(run_ref)

    th, tl = _split2(t_ref[...])
    wh, wl = _split2(wt_ref[...])
    nt = lambda a, b: lax.dot_general(a, b, (((1,), (1,)), ((), ())), preferred_element_type=F32)
    logits = nt(wh, th) + (nt(wh, tl) + nt(wl, th))
    s = jax.nn.sigmoid(logits)
    sel = s + bias_ref[...]
    srow = [s[e:e + 1, :] for e in range(E)]
    row = [sel[e:e + 1, :] for e in range(E)]
    best_g, best_i = None, None
    for gq in range(N_GROUPS):
        a, b, c, d = row[gq * per:(gq + 1) * per]
        m1, n1, m2, n2 = jnp.maximum(a, b), jnp.minimum(a, b), jnp.maximum(c, d), jnp.minimum(c, d)
        score = jnp.maximum(m1, m2) + jnp.maximum(jnp.minimum(m1, m2), jnp.maximum(n1, n2))
        if gq == 0:
            best_g, best_i = score, jnp.zeros_like(score, dtype=jnp.int32)
        else:
            better = score > best_g
            best_g = jnp.where(better, score, best_g)
            best_i = jnp.where(better, gq, best_i)
    masked = [jnp.where(best_i == e // per, row[e], NEG_INF) for e in range(E)]

    def argmax_first(vals):
        top, idx = vals[0], jnp.zeros_like(best_i)
        for e in range(1, E):
            better = vals[e] > top
            top = jnp.where(better, vals[e], top)
            idx = jnp.where(better, e, idx)
        return idx

    i1 = argmax_first(masked)
    i2 = argmax_first([jnp.where(i1 == e, -jnp.inf, masked[e]) for e in range(E)])
    w1 = sum(jnp.where(i1 == e, srow[e], 0.0) for e in range(E))
    w2 = sum(jnp.where(i2 == e, srow[e], 0.0) for e in range(E))
    tot = w1 + w2
    chosen = jnp.concatenate([jnp.where((i1 == e) | (i2 == e), 1.0, 0.0) for e in range(E)], axis=0)
    earlier = lax.broadcasted_iota(jnp.int32, (tm, tm), 0) < lax.broadcasted_iota(jnp.int32, (tm, tm), 1)
    pos = run_ref[:, 0:1] + _bdot(chosen, earlier.astype(BF16))
    run_ref[...] = run_ref[...] + jnp.sum(chosen, axis=1, keepdims=True)
    p1 = sum(jnp.where(i1 == e, pos[e:e + 1, :], 0.0) for e in range(E)).astype(jnp.int32)
    p2 = sum(jnp.where(i2 == e, pos[e:e + 1, :], 0.0) for e in range(E)).astype(jnp.int32)
    zi = jnp.zeros((V7X_SUBLANES - 4, tm), jnp.int32)
    idx_ref[...] = jnp.concatenate([i1, i2, p1, p2, zi], axis=0)
    zf = jnp.zeros((V7X_SUBLANES - 2, tm), F32)
    gate_ref[...] = jnp.concatenate([w1 / tot, w2 / tot, zf], axis=0)
    cnt_ref[...] = run_ref[...]


def moe_route(t, router_w, router_bias, *, tm=512):
    n, D = t.shape
    E = router_w.shape[1]
    tm = min(tm, n)
    row8 = pl.BlockSpec((V7X_SUBLANES, tm), lambda i: (0, i))
    return pl.pallas_call(
        _route_kernel,
        grid=(n // tm,),
        in_specs=[pl.BlockSpec((tm, D), lambda i: (i, 0)),
                  pl.BlockSpec((E, D), lambda i: (0, 0)),
                  pl.BlockSpec((E, 1), lambda i: (0, 0))],
        out_specs=[row8, row8, pl.BlockSpec((E, V7X_LANES), lambda i: (0, 0))],
        out_shape=[jax.ShapeDtypeStruct((V7X_SUBLANES, n), jnp.int32),
                   jax.ShapeDtypeStruct((V7X_SUBLANES, n), F32),
                   jax.ShapeDtypeStruct((E, V7X_LANES), F32)],
        scratch_shapes=[pltpu.VMEM((E, V7X_LANES), F32)],
        compiler_params=_params(("arbitrary",)),
        name="moe_route",
    )(t, router_w.T, router_bias.reshape(E, 1))


MOE_TILE = 1152
COPY_ROWS = 256


def _dispatch_kernel(slot_ref, x_ref, xg_ref, sem):
    n = slot_ref.shape[0] // 2
    base = pl.program_id(0) * COPY_ROWS

    def copy(r, k):
        return pltpu.make_async_copy(x_ref.at[pl.ds(r, 1)], xg_ref.at[pl.ds(slot_ref[k * n + base + r], 1)], sem)

    def start(r, carry):
        copy(r, 0).start()
        copy(r, 1).start()
        return carry

    def wait(r, carry):
        copy(r, 0).wait()
        copy(r, 1).wait()
        return carry

    lax.fori_loop(0, COPY_ROWS, start, 0)
    lax.fori_loop(0, COPY_ROWS, wait, 0)


def moe_dispatch(slots, x_words, n_rows):
    n, W = x_words.shape
    return pl.pallas_call(
        _dispatch_kernel,
        grid_spec=pltpu.PrefetchScalarGridSpec(
            num_scalar_prefetch=1,
            grid=(n // COPY_ROWS,),
            in_specs=[pl.BlockSpec((COPY_ROWS, W), lambda i, s: (i, 0))],
            out_specs=pl.BlockSpec(memory_space=pl.ANY),
            scratch_shapes=[pltpu.SemaphoreType.DMA(())],
        ),
        out_shape=jax.ShapeDtypeStruct((n_rows, W), x_words.dtype),
        compiler_params=_params(("arbitrary",)),
        name="moe_dispatch",
    )(slots, x_words)


def _moe_up_kernel(te_ref, rows_ref, blk_ref, x_ref, wg_ref, wu_ref, o_ref, xs_ref):
    rows = rows_ref[pl.program_id(0)]
    half = x_ref.shape[1]

    @pl.when((rows > 0) & (pl.program_id(1) == 0))
    def _():
        valid = lax.broadcasted_iota(jnp.int32, x_ref.shape, 0) < rows
        lo, hi = _unpack_halves(jnp.where(valid, x_ref[...], jnp.uint32(0)))
        xs_ref[:, :half] = lo
        xs_ref[:, half:] = hi

    @pl.when(rows > 0)
    def _():
        x = xs_ref[...]
        hg = jnp.dot(x, wg_ref[...].astype(BF16), preferred_element_type=F32)
        hu = jnp.dot(x, wu_ref[...].astype(BF16), preferred_element_type=F32)
        o_ref[...] = (hg * jax.nn.sigmoid(hg) * hu).astype(o_ref.dtype)


def _moe_down_kernel(te_ref, rows_ref, blk_ref, a_ref, wd_ref, o_ref):
    @pl.when(rows_ref[pl.program_id(0)] > 0)
    def _():
        o_ref[...] = _bdot(a_ref[...], wd_ref[...])


def moe_experts(xg, w_gate, w_up, w_down, tile_expert, tile_rows, tile_block, *, layer, tm, tf=256, tn=1024):
    _, E, D, F = w_gate.shape
    nt = tile_expert.shape[0]
    sq = pl.Squeezed()
    nf, nc = F // tf, D // tn
    hold = lambda i, last, tr, j: jnp.where(tr[j] > 0, i, last)
    act = pl.pallas_call(
        _moe_up_kernel,
        grid_spec=pltpu.PrefetchScalarGridSpec(
            num_scalar_prefetch=3,
            grid=(nt, nf),
            in_specs=[pl.BlockSpec((tm, D // 2), lambda j, f, te, tr, tb: (tb[j], 0)),
                      pl.BlockSpec((sq, sq, D, tf), lambda j, f, te, tr, tb: (layer, te[j], 0, hold(f, nf - 1, tr, j))),
                      pl.BlockSpec((sq, sq, D, tf), lambda j, f, te, tr, tb: (layer, te[j], 0, hold(f, nf - 1, tr, j)))],
            out_specs=pl.BlockSpec((tm, tf), lambda j, f, te, tr, tb: (tb[j], hold(f, nf - 1, tr, j))),
            scratch_shapes=[pltpu.VMEM((tm, D), BF16)],
        ),
        out_shape=jax.ShapeDtypeStruct((nt * tm, F), BF16),
        compiler_params=_params(("arbitrary", "arbitrary")),
        name="moe_up",
    )(tile_expert, tile_rows, tile_block, xg, w_gate, w_up)
    return pl.pallas_call(
        _moe_down_kernel,
        grid_spec=pltpu.PrefetchScalarGridSpec(
            num_scalar_prefetch=3,
            grid=(nt, nc),
            in_specs=[pl.BlockSpec((tm, F), lambda j, c, te, tr, tb: (tb[j], 0)),
                      pl.BlockSpec((sq, sq, F, tn), lambda j, c, te, tr, tb: (layer, te[j], 0, hold(c, nc - 1, tr, j)))],
            out_specs=pl.BlockSpec((tm, tn), lambda j, c, te, tr, tb: (tb[j], hold(c, nc - 1, tr, j))),
        ),
        out_shape=jax.ShapeDtypeStruct((nt * tm, D), F32),
        compiler_params=_params(("arbitrary", "arbitrary")),
        name="moe_down",
    )(tile_expert, tile_rows, tile_block, act, w_down)


def _combine_kernel(slot_ref, y_ref, gate_ref, o_ref, ya_ref, yb_ref, sem):
    n = slot_ref.shape[0] // 2
    base = pl.program_id(0) * COPY_ROWS

    def copy(r, k):
        dst = ya_ref if k == 0 else yb_ref
        return pltpu.make_async_copy(y_ref.at[pl.ds(slot_ref[k * n + base + r], 1)], dst.at[pl.ds(r, 1)], sem)

    def start(r, carry):
        copy(r, 0).start()
        copy(r, 1).start()
        return carry

    def wait(r, carry):
        copy(r, 0).wait()
        copy(r, 1).wait()
        return carry

    lax.fori_loop(0, COPY_ROWS, start, 0)
    lax.fori_loop(0, COPY_ROWS, wait, 0)
    g = gate_ref[...].T
    o_ref[...] = ya_ref[...] * g[:, 0:1] + yb_ref[...] * g[:, 1:2]


def moe_combine(slots, y, gates):
    n = gates.shape[1]
    D = y.shape[1]
    return pl.pallas_call(
        _combine_kernel,
        grid_spec=pltpu.PrefetchScalarGridSpec(
            num_scalar_prefetch=1,
            grid=(n // COPY_ROWS,),
            in_specs=[pl.BlockSpec(memory_space=pl.ANY),
                      pl.BlockSpec((V7X_SUBLANES, COPY_ROWS), lambda i, s: (0, i))],
            out_specs=pl.BlockSpec((COPY_ROWS, D), lambda i, s: (i, 0)),
            scratch_shapes=[pltpu.VMEM((COPY_ROWS, D), F32), pltpu.VMEM((COPY_ROWS, D), F32),
                            pltpu.SemaphoreType.DMA(())],
        ),
        out_shape=jax.ShapeDtypeStruct((n, D), F32),
        compiler_params=_params(("arbitrary",)),
        name="moe_combine",
    )(slots, y, gates)


def _na_kernel(rpb_ref, q_ref, k_ref, v_ref, kc_ref, vc_ref, o_ref, tb_ref):
    h = pl.program_id(1)
    GW, HD = GRID_W, V7X_LANES
    n_rows = q_ref.shape[1] // GW
    wr = NA_WIN_R
    n_dc = 2 * NA_WIN_C - 1
    qcol = lax.broadcasted_iota(jnp.int32, (GW, 2 * GW), 0)
    lane = lax.broadcasted_iota(jnp.int32, (GW, 2 * GW), 1)
    kcol = lane & (GW - 1)
    upper = lane >= GW
    dci = jnp.clip(kcol - qcol + NA_WIN_C - 1, 0, n_dc - 1)
    cstart = jnp.clip(qcol - NA_WIN_C // 2, 0, GW - NA_WIN_C)
    col_ok = (kcol >= cstart) & (kcol < cstart + NA_WIN_C)
    for dr in range(2 * wr - 2):
        acc = jnp.zeros((GW, 2 * GW), F32)
        for j in range(n_dc):
            lo = rpb_ref[h, dr * n_dc + j]
            hi = rpb_ref[h, (dr + 1) * n_dc + j]
            acc = jnp.where(dci == j, jnp.where(upper, hi, lo), acc)
        tb_ref[dr] = acc
    ok4 = jnp.concatenate([col_ok] * (wr // 2), axis=1)
    kc = kc_ref[0]
    vc = vc_ref[0]
    scale = HD ** -0.5

    def body(r, carry):
        rs = jnp.clip(r - wr // 2, 0, n_rows - wr)
        q = q_ref[0, pl.ds(pl.multiple_of(r * GW, GW), GW), :] * scale
        k0 = pl.multiple_of(rs * GW, GW)
        kw = k_ref[0, pl.ds(k0, wr * GW), :]
        vw = v_ref[0, pl.ds(k0, wr * GW), :]
        dr0 = rs - r + wr - 1
        bias = jnp.concatenate([tb_ref[dr0 + 2 * w] for w in range(wr // 2)], axis=1)
        s_lat = jnp.where(ok4, _bdot_nt(q, kw) + bias, NEG_INF)
        s_ctx = _bdot_nt(q, kc)
        m = jnp.maximum(jnp.max(s_lat, axis=-1, keepdims=True), jnp.max(s_ctx, axis=-1, keepdims=True))
        p_lat = jnp.exp(s_lat - m)
        p_ctx = jnp.exp(s_ctx - m)
        den = jnp.sum(p_lat, axis=-1, keepdims=True) + jnp.sum(p_ctx, axis=-1, keepdims=True)
        o = (_bdot(p_lat, vw) + _bdot(p_ctx, vc)) / den
        o_ref[0, pl.ds(pl.multiple_of(r * GW, GW), GW), :] = o
        return carry

    lax.fori_loop(0, n_rows, body, 0)


def neighbourhood_attention(pl_lat, kv_ctx, rpb):
    B, S, _ = pl_lat.shape
    Lc = kv_ctx.shape[1]
    H, HD = NA_HEADS, V7X_LANES
    lat = lambda off: pl.BlockSpec((1, S, HD), lambda b, h: (b, 0, off + h))
    cx = lambda off: pl.BlockSpec((1, Lc, HD), lambda b, h: (b, 0, off + h))
    return pl.pallas_call(
        _na_kernel,
        grid=(B, H),
        in_specs=[pl.BlockSpec(memory_space=pltpu.SMEM), lat(0), lat(H), lat(2 * H), cx(0), cx(H)],
        out_specs=pl.BlockSpec((1, S, HD), lambda b, h: (b, 0, h)),
        out_shape=jax.ShapeDtypeStruct((B, S, H * HD), F32),
        scratch_shapes=[pltpu.VMEM((2 * NA_WIN_R - 2, GRID_W, 2 * GRID_W), F32)],
        compiler_params=_params(("parallel", "parallel")),
        name="neighbourhood_attention",
    )(rpb.reshape(H, -1), pl_lat, pl_lat, pl_lat, kv_ctx, kv_ctx)


DFT_N1 = 64
DFT_N2 = 128
DFT_SUB = 8


@functools.lru_cache(maxsize=None)
def _dft_constants():
    n1, n2 = DFT_N1, DFT_N2
    n = n1 * n2
    a = np.arange(n1)
    f1 = np.exp(-2j * np.pi * np.outer(a, a) / n1)
    half = f1[:, :n1 // 2]
    eye = np.eye(DFT_SUB)
    fwd = np.kron(half, eye)
    inv = np.kron(np.conj(f1)[:n1 // 2, :], eye) / n
    b = np.arange(n2)
    f2 = np.exp(-2j * np.pi * np.outer(b, b) / n2)
    tw = np.exp(-2j * np.pi * np.outer(a, b) / n)
    m = f2[None, :, :] * tw[:, None, :]
    mt = np.transpose(m, (0, 2, 1))
    c = lambda z: (np.ascontiguousarray(z.real, dtype=np.float32), np.ascontiguousarray(z.imag, dtype=np.float32))
    return c(fwd) + c(inv) + c(m) + c(mt)


def _hyena_conv_kernel(hf_ref, hb_ref, z_ref, fwr_ref, fwi_ref, ivr_ref, ivi_ref,
                       mr_ref, mi_ref, mtr_ref, mti_ref, o_ref, ar_ref, ai_ref, kr_ref, ki_ref):
    L, TC = hf_ref.shape
    N1, N2, SUB = DFT_N1, DFT_N2, DFT_SUB
    nblk = N2 // SUB

    def stage1(load):
        def blk(nb, carry):
            off = pl.multiple_of(nb * SUB, SUB)
            xs = jnp.concatenate([load(n1 * N2 + off) for n1 in range(N1 // 2)], axis=0)
            yr = _bdot(fwr_ref[...], xs)
            yi = _bdot(fwi_ref[...], xs)
            for k1 in range(N1):
                ar_ref[pl.ds(k1 * N2 + off, SUB), :] = yr[k1 * SUB:(k1 + 1) * SUB]
                ai_ref[pl.ds(k1 * N2 + off, SUB), :] = yi[k1 * SUB:(k1 + 1) * SUB]
            return carry

        lax.fori_loop(0, nblk, blk, 0)

    def spectrum(k1):
        rows = pl.ds(pl.multiple_of(k1 * N2, N2), N2)
        a_r, a_i = ar_ref[rows, :], ai_ref[rows, :]
        m_r, m_i = mr_ref[k1], mi_ref[k1]
        return rows, _bdot(m_r, a_r) - _bdot(m_i, a_i), _bdot(m_r, a_i) + _bdot(m_i, a_r)

    stage1(lambda r0: hf_ref[pl.ds(r0, SUB), :])

    def filt_f(k1, carry):
        rows, x_r, x_i = spectrum(k1)
        kr_ref[rows, :] = x_r
        ki_ref[rows, :] = x_i
        return carry

    lax.fori_loop(0, N1, filt_f, 0)
    sub_iota = lax.broadcasted_iota(jnp.int32, (SUB, TC), 0)
    stage1(lambda r0: jnp.where(sub_iota + r0 == 0, 0.0, hb_ref[pl.ds(r0, SUB), :]))

    def filt_b(k1, carry):
        rows, x_r, x_i = spectrum(k1)
        kr_ref[rows, :] += x_r
        ki_ref[rows, :] -= x_i
        return carry

    lax.fori_loop(0, N1, filt_b, 0)

    for b in range(z_ref.shape[0]):
        stage1(lambda r0: z_ref[b, pl.ds(r0, SUB), :])

        def conv(k1, carry):
            rows, x_r, x_i = spectrum(k1)
            f_r, f_i = kr_ref[rows, :], ki_ref[rows, :]
            y_r = x_r * f_r - x_i * f_i
            y_i = x_r * f_i + x_i * f_r
            t_r, t_i = mtr_ref[k1], mti_ref[k1]
            ar_ref[rows, :] = _bdot(t_r, y_r) + _bdot(t_i, y_i)
            ai_ref[rows, :] = _bdot(t_r, y_i) - _bdot(t_i, y_r)
            return carry

        lax.fori_loop(0, N1, conv, 0)
        def last(nb, carry):
            off = pl.multiple_of(nb * SUB, SUB)
            br = jnp.concatenate([ar_ref[pl.ds(k1 * N2 + off, SUB), :] for k1 in range(N1)], axis=0)
            bi = jnp.concatenate([ai_ref[pl.ds(k1 * N2 + off, SUB), :] for k1 in range(N1)], axis=0)
            y = _bdot(ivr_ref[...], br) - _bdot(ivi_ref[...], bi)
            for n1 in range(N1 // 2):
                o_ref[b, pl.ds(n1 * N2 + off, SUB), :] = y[n1 * SUB:(n1 + 1) * SUB]
            return carry

        lax.fori_loop(0, nblk, last, 0)


def hyena_long_conv(z, h_f, h_b, *, tc=V7X_LANES):
    B, L, C = z.shape
    assert L == DFT_N1 * DFT_N2 // 2
    consts = [jnp.asarray(a, BF16) for a in _dft_constants()]
    whole = lambda a: pl.BlockSpec(a.shape, lambda j: (0,) * a.ndim)
    col = pl.BlockSpec((L, tc), lambda j: (0, j))
    bcol = pl.BlockSpec((B, L, tc), lambda j: (0, 0, j))
    return pl.pallas_call(
        _hyena_conv_kernel,
        grid=(C // tc,),
        in_specs=[col, col, bcol] + [whole(a) for a in consts],
        out_specs=bcol,
        out_shape=jax.ShapeDtypeStruct((B, L, C), F32),
        scratch_shapes=[pltpu.VMEM((2 * L, tc), F32)] * 4,
        compiler_params=_params(("parallel",)),
        name="hyena_long_conv",
    )(h_f, h_b, z, *consts)


def _pick(n, options):
    for t in options:
        if n % t == 0:
            return t
    raise ValueError(f"no tile for {n} in {options}")


def _mm(a, w, **kw):
    tm = _pick(a.shape[0], (1088, 1024, 768, 512, 256, 8))
    tn = _pick(w.shape[1], (512, 640, 384, 256, 128))
    return matmul(a, w, tm=tm, tn=tn, **kw)


def _pad_to(a, axis, mult):
    pad = (-a.shape[axis]) % mult
    if pad == 0:
        return a
    widths = [(0, 0)] * a.ndim
    widths[axis] = (0, pad)
    return jnp.pad(a, widths)


def _silu(x):
    return x * jax.nn.sigmoid(x)


def _seq_parts(x, lc, fn):
    if lc == 0:
        return fn(x)
    return jnp.concatenate([fn(x[:, :lc]), fn(x[:, lc:])], axis=1)


def _dwconv(x, w):
    k, length = w.shape[0], x.shape[1]
    xp = jnp.pad(x, ((0, 0), (k // 2, k // 2), (0, 0)))
    return sum(xp[:, j:j + length] * w[j] for j in range(k))


def _centred_shift(p):
    prev = jnp.pad(p, ((0, 0), (1, 0), (0, 0)))[:, :-1]
    nxt = jnp.pad(p, ((0, 0), (0, 1), (0, 0)))[:, 1:]
    return 0.5 * (prev + nxt)


def _l2norm(t):
    return t * lax.rsqrt(jnp.sum(t * t, -1, keepdims=True) + 1e-6)


def _axial_rope(t):
    length, d = t.shape[1], t.shape[-1]
    nf = d // 4
    pos = jnp.arange(length)
    inv = ROPE_BASE ** (-jnp.arange(nf, dtype=F32) / nf)

    def cs(p):
        ang = p.astype(F32)[:, None] * inv[None, :]
        return jnp.cos(ang), jnp.sin(ang)

    cr, sr = cs(pos // GRID_W)
    cc, sc = cs(pos % GRID_W)
    cos = jnp.concatenate([cr, cr, cc, cc], -1)[None, :, None, :]
    sin = jnp.concatenate([-sr, sr, -sc, sc], -1)[None, :, None, :]
    swapped = jnp.flip(t.reshape(*t.shape[:-1], 2, 2, nf), axis=-2).reshape(t.shape)
    return t * cos + swapped * sin


def _mixer_ab(h, lc, w_in, dn_conv_w, dn_a_log, dn_dt_bias, dn_norm_w,
              rw_mu, rw_w0, rw_w2, rw_a0, rw_a2, rw_g2, rw_k_k, rw_k_a, rw_r_k, rw_ln_w, rw_ln_b, w_out):
    B, T, D = h.shape
    mw = D // 2
    hd_dn = mw // DN_HEADS
    h_rw = mw // RW_HD
    p_a = 4 * mw + 4 * DN_HEADS
    h2 = h.reshape(B * T, D)
    pa = _mm(h2, _pad_to(w_in[:, :p_a], 1, V7X_LANES).astype(BF16)).reshape(B, T, -1)
    pb = _mm(h2, _pad_to(w_in[:, p_a:], 1, V7X_LANES).astype(BF16)).reshape(B, T, -1)

    qkv = _seq_parts(pa[..., :3 * mw], lc, lambda u: _silu(_dwconv(u, dn_conv_w)))
    q, k, v = (t.reshape(B, T, DN_HEADS, hd_dn) for t in jnp.split(qkv, 3, axis=-1))
    rope_lat = lambda t: jnp.concatenate([t[:, :lc], _axial_rope(t[:, lc:])], axis=1)
    q, k = rope_lat(_l2norm(q)), rope_lat(_l2norm(k))
    q = q * hd_dn ** -0.5
    z = pa[..., 3 * mw:4 * mw]
    bg = pa[..., 4 * mw:p_a].reshape(B, T, 4, DN_HEADS)
    beta = jax.nn.sigmoid(bg[:, :, :2])
    g = -jnp.exp(dn_a_log) * jax.nn.softplus(bg[:, :, 2:] + dn_dt_bias)
    flat = lambda t: t.reshape(B, T, mw)
    o = 0.0
    for d in range(2):
        bk = k * beta[:, :, d, :, None]
        gd = jnp.broadcast_to(g[:, :, d, :, None], k.shape)
        o = o + delta_scan(flat(q), flat(k), None, flat(bk), flat(v), flat(gd),
                           head_dim=hd_dn, reverse=d == 1, scalar_decay=True, beta_excl=False)
    o = o.reshape(B, T, DN_HEADS, hd_dn)
    o = o * lax.rsqrt(jnp.mean(o * o, -1, keepdims=True) + NORM_EPS) * dn_norm_w
    dn = flat(o * _silu(z).reshape(o.shape))

    p_b = w_in.shape[1] - p_a
    p = pb[..., :p_b]
    p = p + (_seq_parts(p, lc, _centred_shift) - p) * rw_mu
    r, k, v = p[..., :mw], p[..., mw:2 * mw], p[..., 2 * mw:3 * mw]
    o1 = 3 * mw
    n_w, n_a = rw_w2.shape[1], rw_a2.shape[1]
    o2 = o1 + 2 * n_w
    o3 = o2 + 2 * n_a
    wl = p[..., o1:o2].reshape(B * T, 2, n_w)
    al = p[..., o2:o3].reshape(B * T, 2, n_a)
    gl = p[..., o3:].reshape(B * T, -1)
    gate = _mm(_pad_to(jax.nn.sigmoid(gl), 1, V7X_LANES), _pad_to(rw_g2, 0, V7X_LANES)).reshape(B, T, mw)
    heads = lambda t: t.reshape(B, T, h_rw, RW_HD)
    kk = flat(_l2norm(heads(k * rw_k_k)))
    y = 0.0
    for d in range(2):
        w = -jax.nn.softplus(-(rw_w0[d] + _mm(jnp.tanh(wl[:, d]), rw_w2[d]).reshape(B, T, mw))) - 0.5
        a = jax.nn.sigmoid(rw_a0[d] + _mm(al[:, d], rw_a2[d]).reshape(B, T, mw))
        kd = k * (1.0 + (a - 1.0) * rw_k_a)
        y = y + delta_scan(r, kk, -(kk * a), kd, v, -jnp.exp(w),
                           head_dim=RW_HD, reverse=d == 1, scalar_decay=False, beta_excl=True)
    y = heads(y)
    mu = jnp.mean(y, -1, keepdims=True)
    var = jnp.mean(jnp.square(y - mu), -1, keepdims=True)
    yn = flat((y - mu) * lax.rsqrt(var + RW_LN_EPS)) * rw_ln_w + rw_ln_b
    bonus = flat(jnp.sum(heads(r) * heads(k) * rw_r_k, -1, keepdims=True) * heads(v))
    rw = (yn + bonus) * gate

    cat = jnp.concatenate([dn, rw], -1).astype(BF16).reshape(B * T, D)
    return _mm(cat, w_out.astype(BF16)).reshape(B, T, D)


def _hyena_filters(length, ch, w1, b1, freq, w2, b2, w3, b3, w4):
    t = jnp.linspace(0.0, 1.0, length, dtype=F32)
    bands = (HY_EMB - 1) // 2
    wpos = 2 * math.pi * jnp.arange(length, dtype=F32) / length
    fb = jnp.linspace(1e-4, bands - 1, bands, dtype=F32)
    ang = wpos[:, None] * fb[None, :]
    z = jnp.concatenate([t[:, None], jnp.cos(ang), -jnp.sin(ang)], axis=-1)
    dense = lambda u, w: _mm(_pad_to(u, 1, V7X_LANES), _pad_to(_pad_to(w, 0, V7X_LANES), 1, V7X_LANES),
                             passes=3)[:, :w.shape[1]]
    h = jnp.sin(freq * (dense(z, w1) + b1))
    h = jnp.sin(freq * (dense(h, w2) + b2))
    h = jnp.sin(freq * (dense(h, w3) + b3))
    h = dense(h, w4).reshape(length, 2, ch)
    deltas = jnp.abs(jnp.linspace(HY_MIN_DECAY, HY_MAX_DECAY, ch, dtype=F32))
    window = jnp.exp(-t[:, None] * deltas[None, :])
    return h[:, 0] * window, h[:, 1] * window


def _mixer_cd(hc, hl, w_in, na_rpb, hy_conv_w, hy_conv_b, filt, hy_skip, w_out):
    B, S, D = hl.shape
    lc = hc.shape[1]
    mw = D // 2
    w_bf = w_in.astype(BF16)
    pl_ = _mm(hl.reshape(B * S, D), w_bf).reshape(B, S, -1)
    kv_c = _mm(hc.reshape(B * lc, D), w_bf[:, mw:3 * mw]).reshape(B, lc, 2 * mw)
    att = neighbourhood_attention(pl_, kv_c, na_rpb)
    u = _dwconv(pl_[..., 3 * mw:], hy_conv_w) + hy_conv_b
    x0, x1, v = jnp.split(u, 3, axis=-1)
    h_f, h_b = _hyena_filters(S, mw, *filt)
    z = x1 * v
    hy = x0 * (hyena_long_conv(z, h_f, h_b) + z * hy_skip)
    cat = jnp.concatenate([att, hy], -1).astype(BF16).reshape(B * S, D)
    return _mm(cat, w_out.astype(BF16)).reshape(B, S, D)


def _moe(t_packed, t_f32, router_w, router_bias, w_gate, w_up, w_down, layer, tile=MOE_TILE):
    n, D = t_f32.shape
    E = w_gate.shape[1]
    idx, gates, counts = moe_route(t_f32, router_w, router_bias)
    cnt = counts[:, 0].astype(jnp.int32)
    nt = -(-2 * n // tile) + E
    tiles = (cnt + tile - 1) // tile
    ends = jnp.cumsum(tiles)
    starts = ends - tiles
    used = ends[-1]
    j = jnp.arange(nt, dtype=jnp.int32)
    jj = jnp.minimum(j, used - 1)
    tile_expert = jnp.sum((jj[:, None] >= ends[None, :]).astype(jnp.int32), axis=1)
    tile_rows = jnp.where(j < used, jnp.clip(cnt[tile_expert] - (jj - starts[tile_expert]) * tile, 0, tile), 0)
    row0 = starts * tile
    slots = jnp.concatenate([row0[idx[0]] + idx[2], row0[idx[1]] + idx[3]]).astype(jnp.int32)
    xg = moe_dispatch(slots, t_packed, nt * tile)
    y = moe_experts(xg, w_gate, w_up, w_down, tile_expert.astype(jnp.int32), tile_rows.astype(jnp.int32), jj,
                    layer=layer, tm=tile)
    return moe_combine(slots, y, gates)


def kernel(x, c, ctx, c_ctx, ada_w, ada_b, norm1_g, norm2_g, final_g, ab_w_in, dn_conv_w, dn_a_log, dn_dt_bias, dn_norm_w, rw_mu, rw_w0, rw_w2, rw_a0, rw_a2, rw_g2, rw_k_k, rw_k_a, rw_r_k, rw_ln_w, rw_ln_b, ab_w_out, cd_w_in, na_rpb, hy_conv_w, hy_conv_b, hy_w1, hy_b1, hy_freq, hy_w2, hy_b2, hy_w3, hy_b3, hy_w4, hy_skip, cd_w_out, router_w, router_bias, moe_w_gate, moe_w_up, moe_w_down):
    B, S, D = x.shape
    lc = ctx.shape[1]
    assert ada_w.shape[0] == 2 and lc == ROW_TILE and S % ROW_TILE == 0 and B + 1 <= V7X_SUBLANES
    T = lc + S
    ctx_tiles = lc // ROW_TILE
    cond = jnp.concatenate([c, c_ctx[None], jnp.zeros((V7X_SUBLANES - B - 1, D), F32)], axis=0)
    mods = [matmul(cond, ada_w, tm=V7X_SUBLANES, tn=512, bias=ada_b[l], a_silu=True, layer=l,
                   name="adaln")[:B + 1].reshape(B + 1, 6, D) for l in range(2)]

    def rows(m, idx):
        return jnp.concatenate([jnp.broadcast_to(m[B, idx], (B, lc, D)),
                                jnp.broadcast_to(m[:B, idx][:, None, :], (B, S, D))], axis=1)

    m = mods[0]
    xa = jnp.concatenate([ctx, x], axis=1)
    h = norm_mod(xa, norm1_g[0], m, which=0, ctx_tiles=ctx_tiles)
    mix = _mixer_ab(h, lc, ab_w_in[0], dn_conv_w[0], dn_a_log[0], dn_dt_bias[0], dn_norm_w[0],
                    rw_mu[0], rw_w0[0], rw_w2[0], rw_a0[0], rw_a2[0], rw_g2[0], rw_k_k[0], rw_k_a[0],
                    rw_r_k[0], rw_ln_w[0], rw_ln_b[0], ab_w_out[0])
    xa = xa + rows(m, 2) * mix
    h_pk, h_f32 = norm_mod(xa, norm2_g[0], m, which=1, ctx_tiles=ctx_tiles, kinds=("packed", "f32"))
    f = _moe(h_pk.reshape(B * T, D // 2), h_f32.reshape(B * T, D), router_w, router_bias,
             moe_w_gate, moe_w_up, moe_w_down, 0).reshape(B, T, D)
    xa = xa + rows(m, 5) * f

    m = mods[1]
    cx, xl = xa[:, :lc], xa[:, lc:]
    hl = norm_mod(xl, norm1_g[1], m, which=0, ctx_tiles=0)
    hc = norm_mod(cx, norm1_g[1], m, which=0, ctx_tiles=ctx_tiles)
    filt = (hy_w1[0], hy_b1[0], hy_freq[0], hy_w2[0], hy_b2[0], hy_w3[0], hy_b3[0], hy_w4[0])
    ml = _mixer_cd(hc, hl, cd_w_in[0], na_rpb[0], hy_conv_w[0], hy_conv_b[0], filt, hy_skip[0], cd_w_out[0])
    xl = xl + m[:B, 2][:, None, :] * ml
    h_pk, h_f32 = norm_mod(xl, norm2_g[1], m, which=1, ctx_tiles=0, kinds=("packed", "f32"))
    f = _moe(h_pk.reshape(B * S, D // 2), h_f32.reshape(B * S, D), router_w, router_bias,
             moe_w_gate, moe_w_up, moe_w_down, 1).reshape(B, S, D)
    xl = xl + m[:B, 5][:, None, :] * f
    return xl * lax.rsqrt(jnp.mean(xl * xl, -1, keepdims=True) + NORM_EPS) * final_g
```

```python
import functools
import math

import jax
import jax.numpy as jnp
import numpy as np
from jax import lax
from jax.experimental import pallas as pl
from jax.experimental.pallas import tpu as pltpu

F32 = jnp.float32
BF16 = jnp.bfloat16

V7X_LANES = 128
V7X_SUBLANES = 8
V7X_MXU_DIM = 256
V7X_VMEM_LIMIT_BYTES = 56 * 1024 * 1024

GRID_W = 64
NORM_EPS = 1e-6
NEG_INF = -1e30
ROPE_BASE = 10000.0
DN_HEADS = 16
DN_CONV = 5
RW_HD = 64
RW_LN_EPS = 64e-5
NA_HEADS = 16
NA_WIN_R = 8
NA_WIN_C = 16
NA_LOCK = 4
HY_EMB = 33
HY_TARGET = 1e-2
HY_MAX_DECAY = math.log(HY_TARGET) / 0.3
HY_MIN_DECAY = math.log(HY_TARGET) / 1.5
N_EXPERTS = 16
N_GROUPS = 4

ROW_TILE = 256
SCAN_CHUNK = 64


def _params(semantics, vmem=V7X_VMEM_LIMIT_BYTES):
    return pltpu.CompilerParams(dimension_semantics=semantics, vmem_limit_bytes=vmem)


def _bdot(a, b):
    return jnp.dot(a.astype(BF16), b.astype(BF16), preferred_element_type=F32)


def _bdot_nt(a, b):
    return lax.dot_general(a.astype(BF16), b.astype(BF16), (((1,), (1,)), ((), ())),
                           preferred_element_type=F32)


def _split2(x):
    hi = x.astype(BF16)
    lo = (x - hi.astype(F32)).astype(BF16)
    return hi, lo


def _split3(x):
    hi = x.astype(BF16)
    r1 = x - hi.astype(F32)
    mid = r1.astype(BF16)
    lo = (r1 - mid.astype(F32)).astype(BF16)
    return hi, mid, lo


def _dot3(a, b):
    ah, al = _split2(a)
    bh, bl = _split2(b)
    d = lambda u, v: jnp.dot(u, v, preferred_element_type=F32)
    return d(ah, bh) + (d(ah, bl) + d(al, bh))


def _dot_exact_left(m, x):
    hi, mid, lo = _split3(x)
    d = lambda v: jnp.dot(m, v, preferred_element_type=F32)
    return d(hi) + (d(mid) + d(lo))


def _unit_tri_inverse(mats, row, col):
    eye = (row == col).astype(F32)
    in8 = (row >> 3) == (col >> 3)
    a8 = [jnp.where(in8, a, 0.0) for a in mats]
    a2 = [_bdot(u, u) for u in a8]
    p = [eye + u for u in a8]
    p = [u + _bdot(u, w) for u, w in zip(p, a2)]
    a4 = [_bdot(u, u) for u in a2]
    x = [u + _bdot(u, w) for u, w in zip(p, a4)]
    for sh in (3, 4, 5):
        m = ((row >> (sh + 1)) == (col >> (sh + 1))) & ((row >> sh) != (col >> sh))
        left = [_bdot(u, jnp.where(m, a, 0.0)) for u, a in zip(x, mats)]
        x = [u + _bdot(w, u) for u, w in zip(x, left)]
    return x


def _delta_scan_kernel(*refs, lane_groups, head_dim, reverse, scalar_decay, beta_excl, ka_is_neg_kk):
    if ka_is_neg_kk:
        r_ref, kb_ref, kk_ref, v_ref, g_ref, y_ref, t_ref = refs
        ka_ref = None
    else:
        r_ref, kb_ref, ka_ref, kk_ref, v_ref, g_ref, y_ref, t_ref = refs
    TT, W, C = ROW_TILE, V7X_MXU_DIM, SCAN_CHUNK
    n_heads = W // head_dim
    hshift = int(math.log2(head_dim))
    groups = range(lane_groups)
    units = [(gs, j) for gs in groups for j in range(n_heads)]

    @pl.when(pl.program_id(2) == 0)
    def _():
        t_ref[...] = jnp.zeros_like(t_ref)

    row = lax.broadcasted_iota(jnp.int32, (TT, TT), 0)
    col = lax.broadcasted_iota(jnp.int32, (TT, TT), 1)
    same = (row >> 6) == (col >> 6)
    if reverse:
        strict, incl = same & (col > row), same & (col >= row)
    else:
        strict, incl = same & (col < row), same & (col <= row)
    incl_bf, same_bf = incl.astype(BF16), same.astype(BF16)
    load = lambda ref: [ref[0, :, gs * W:(gs + 1) * W] for gs in groups]

    g = load(g_ref)
    gi = [_dot_exact_left(incl_bf, u) for u in g]
    gc = [_dot_exact_left(same_bf, u) for u in g]
    gb = [u - w for u, w in zip(gi, g)] if beta_excl else gi
    r, kb, kk, v = load(r_ref), load(kb_ref), load(kk_ref), load(v_ref)
    ka = [-u for u in kk] if ka_is_neg_kk else load(ka_ref)
    e_rest = [jnp.exp(c - i) for c, i in zip(gc, gi)]
    r_s = [u * jnp.exp(i) for u, i in zip(r, gi)]
    kb_s = [u * jnp.exp(i) for u, i in zip(kb, gb)]
    ka_e = [u * e for u, e in zip(ka, e_rest)]
    kk_e = [u * e for u, e in zip(kk, e_rest)]
    if scalar_decay:
        assert head_dim == V7X_LANES
        gram_r, gram_b, gram_k = r, kb, kk
        gram_a = None if ka_is_neg_kk else ka
    else:
        inv = [jnp.exp(-i) for i in gi]
        gram_r, gram_b = r_s, kb_s
        gram_k = [u * w for u, w in zip(kk, inv)]
        gram_a = None if ka_is_neg_kk else [u * w for u, w in zip(ka, inv)]

    lane_head = lax.broadcasted_iota(jnp.int32, (TT, W), 1) >> hshift
    mj = [lane_head == j for j in range(n_heads)]
    bl = [jnp.where(mj[j], gram_b[gs], 0.0) for gs, j in units]
    rl = [jnp.where(mj[j], gram_r[gs], 0.0) for gs, j in units]
    a_bk = [_bdot_nt(bl[u], gram_k[gs]) for u, (gs, j) in enumerate(units)]
    a_rk = [_bdot_nt(rl[u], gram_k[gs]) for u, (gs, j) in enumerate(units)]
    if ka_is_neg_kk:
        a_ba, a_ra = [-u for u in a_bk], [-u for u in a_rk]
    else:
        a_ba = [_bdot_nt(bl[u], gram_a[gs]) for u, (gs, j) in enumerate(units)]
        a_ra = [_bdot_nt(rl[u], gram_a[gs]) for u, (gs, j) in enumerate(units)]
    if scalar_decay:
        for u, (gs, j) in enumerate(units):
            sl = slice(j * head_dim, (j + 1) * head_dim)
            gi_m = jnp.concatenate([gi[gs][:, sl]] * n_heads, axis=1)
            gb_m = jnp.concatenate([gb[gs][:, sl]] * n_heads, axis=1)
            gi_t = gi_m.T
            d_b = jnp.exp(jnp.minimum(gb_m - gi_t, 0.0))
            d_r = jnp.exp(jnp.minimum(gi_m - gi_t, 0.0))
            a_ba[u], a_bk[u], a_ra[u], a_rk[u] = a_ba[u] * d_b, a_bk[u] * d_b, a_ra[u] * d_r, a_rk[u] * d_r
    a_ba = [jnp.where(strict, u, 0.0) for u in a_ba]
    a_bk = [jnp.where(strict, u, 0.0) for u in a_bk]
    a_ra = [jnp.where(incl, u, 0.0) for u in a_ra]
    a_rk = [jnp.where(incl, u, 0.0) for u in a_rk]
    x = _unit_tri_inverse(a_ba, row, col)
    bj = [_bdot(x[u], kb_s[gs]) for u, (gs, j) in enumerate(units)]
    t1 = [_bdot(a_bk[u], v[gs]) for u, (gs, j) in enumerate(units)]
    wj = [_bdot(x[u], t1[u]) for u in range(len(units))]
    ra = [_bdot(a_ra[u], bj[u]) for u in range(len(units))]
    ya = [_bdot(a_ra[u], wj[u]) for u in range(len(units))]
    yb = [_bdot(a_rk[u], v[gs]) for u, (gs, j) in enumerate(units)]
    b_hat, w1, r_hat, y0 = [], [], [], []
    for gs in groups:
        bh = wh = yh = jnp.zeros((TT, W), F32)
        rh = r_s[gs]
        for j in range(n_heads):
            u = gs * n_heads + j
            bh = jnp.where(mj[j], bj[u], bh)
            wh = jnp.where(mj[j], wj[u], wh)
            rh = rh + jnp.where(mj[j], ra[u], 0.0)
            yh = jnp.where(mj[j], ya[u] + yb[u], yh)
        b_hat.append(bh)
        w1.append(wh)
        r_hat.append(rh)
        y0.append(yh)

    rw = lax.broadcasted_iota(jnp.int32, (W, W), 0)
    cw = lax.broadcasted_iota(jnp.int32, (W, W), 1)
    blockdiag = (rw >> hshift) == (cw >> hshift)
    eye_w = rw == cw
    t = [t_ref[gs] for gs in groups]
    for c in (range(TT // C - 1, -1, -1) if reverse else range(TT // C)):
        sl = slice(c * C, (c + 1) * C)
        for gs in groups:
            y_ref[0, sl, gs * W:(gs + 1) * W] = _bdot(r_hat[gs][sl], t[gs]) + y0[gs][sl]
        ka_t = [ka_e[gs][sl].T for gs in groups]
        kk_t = [kk_e[gs][sl].T for gs in groups]
        decay_col = [jnp.sum(jnp.where(eye_w, jnp.exp(gc[gs][c * C:c * C + 1, :]), 0.0), axis=1, keepdims=True)
                     for gs in groups]
        mix = [jnp.where(blockdiag, _bdot(ka_t[gs], b_hat[gs][sl]), 0.0) for gs in groups]
        add = [jnp.where(blockdiag, _bdot(ka_t[gs], w1[gs][sl]) + _bdot(kk_t[gs], v[gs][sl]), 0.0) for gs in groups]
        t = [decay_col[gs] * t[gs] + _bdot(mix[gs], t[gs]) + add[gs] for gs in groups]
    for gs in groups:
        t_ref[gs] = t[gs]


def delta_scan(r, kb, ka, kk, v, g, *, head_dim, reverse, scalar_decay, beta_excl, lane_groups=2):
    B, T, HW = r.shape
    n_t = T // ROW_TILE
    width = lane_groups * V7X_MXU_DIM
    if reverse:
        tile = lambda i: jnp.where(i == 0, 0, n_t - i)
    else:
        tile = lambda i: i
    spec = pl.BlockSpec((1, ROW_TILE, width), lambda b, h, i: (b, tile(i), h))
    kern = functools.partial(_delta_scan_kernel, lane_groups=lane_groups, head_dim=head_dim, reverse=reverse,
                             scalar_decay=scalar_decay, beta_excl=beta_excl, ka_is_neg_kk=ka is None)
    args = [a for a in (r, kb, ka, kk, v, g) if a is not None]
    return pl.pallas_call(
        kern,
        grid=(B, HW // width, n_t),
        in_specs=[spec] * len(args),
        out_specs=spec,
        out_shape=jax.ShapeDtypeStruct((B, T, HW), F32),
        scratch_shapes=[pltpu.VMEM((lane_groups, V7X_MXU_DIM, V7X_MXU_DIM), F32)],
        compiler_params=_params(("parallel", "parallel", "arbitrary")),
        name="delta_scan_rev" if reverse else "delta_scan_fwd",
    )(*args)


def _matmul_kernel(*refs, nk, passes, has_bias, a_silu):
    a_ref, b_ref = refs[0], refs[1]
    bias_ref = refs[2] if has_bias else None
    o_ref = refs[2 + has_bias]
    acc_ref = refs[3 + has_bias] if nk > 1 else None
    a = a_ref[...]
    if a_silu:
        a = a.astype(F32)
        a = a * jax.nn.sigmoid(a)
    if passes == 3:
        part = _dot3(a.astype(F32), b_ref[...].astype(F32))
    else:
        part = _bdot(a, b_ref[...])

    def finish(acc):
        if has_bias:
            acc = acc + bias_ref[...]
        o_ref[...] = acc.astype(o_ref.dtype)

    if nk == 1:
        finish(part)
    else:
        k = pl.program_id(2)

        @pl.when(k == 0)
        def _():
            acc_ref[...] = part

        @pl.when(k > 0)
        def _():
            acc_ref[...] += part

        @pl.when(k == nk - 1)
        def _():
            finish(acc_ref[...])


def matmul(a, b, *, tm, tn, tk=None, bias=None, out_dtype=F32, passes=1, a_silu=False, layer=None,
           name="matmul"):
    M, K = a.shape
    N = b.shape[-1]
    tk = K if tk is None else tk
    assert M % tm == 0 and N % tn == 0 and K % tk == 0, (a.shape, b.shape, tm, tn, tk)
    nk = K // tk
    if layer is None:
        b_spec = pl.BlockSpec((tk, tn), lambda i, j, k: (k, j))
    else:
        b_spec = pl.BlockSpec((pl.Squeezed(), tk, tn), lambda i, j, k: (layer, k, j))
    in_specs = [pl.BlockSpec((tm, tk), lambda i, j, k: (i, k)), b_spec]
    args = [a, b]
    if bias is not None:
        in_specs.append(pl.BlockSpec((1, tn), lambda i, j, k: (0, j)))
        args.append(bias.reshape(1, N).astype(F32))
    kern = functools.partial(_matmul_kernel, nk=nk, passes=passes, has_bias=bias is not None, a_silu=a_silu)
    return pl.pallas_call(
        kern,
        grid=(M // tm, N // tn, nk),
        in_specs=in_specs,
        out_specs=pl.BlockSpec((tm, tn), lambda i, j, k: (i, j)),
        out_shape=jax.ShapeDtypeStruct((M, N), out_dtype),
        scratch_shapes=[pltpu.VMEM((tm, tn), F32)] if nk > 1 else [],
        compiler_params=_params(("parallel", "parallel", "arbitrary")),
        name=name,
    )(*args)


def _pack_halves(h):
    half = h.shape[1] // 2
    bits = lax.bitcast_convert_type(h.astype(BF16).astype(F32), jnp.uint32)
    return (bits[:, half:] & jnp.uint32(0xFFFF0000)) | (bits[:, :half] >> 16)


def _unpack_halves(w):
    lo = lax.bitcast_convert_type(w << 16, F32).astype(BF16)
    hi = lax.bitcast_convert_type(w & jnp.uint32(0xFFFF0000), F32).astype(BF16)
    return lo, hi


def _norm_mod_kernel(x_ref, g_ref, mod_ref, *o_refs, which, kinds):
    x = x_ref[0]
    y = x * lax.rsqrt(jnp.mean(x * x, axis=-1, keepdims=True) + NORM_EPS) * g_ref[...]
    shift = mod_ref[0, 3 * which:3 * which + 1, :]
    scale = mod_ref[0, 3 * which + 1:3 * which + 2, :]
    h = y * (1.0 + scale) + shift
    for kind, o_ref in zip(kinds, o_refs):
        o_ref[0] = _pack_halves(h) if kind == "packed" else h.astype(o_ref.dtype)


def norm_mod(x, g, mod, *, which, ctx_tiles, kinds=("bf16",)):
    B, T, D = x.shape
    ctx_row = mod.shape[0] - 1
    sel = lambda b, j: jnp.where(j < ctx_tiles, ctx_row, b)
    tile = lambda w: pl.BlockSpec((1, ROW_TILE, w), lambda b, j: (b, j, 0))
    shapes = {"bf16": (D, BF16), "f32": (D, F32), "packed": (D // 2, jnp.uint32)}
    outs = pl.pallas_call(
        functools.partial(_norm_mod_kernel, which=which, kinds=kinds),
        grid=(B, T // ROW_TILE),
        in_specs=[tile(D),
                  pl.BlockSpec((1, D), lambda b, j: (0, 0)),
                  pl.BlockSpec((1, 6, D), lambda b, j: (sel(b, j), 0, 0))],
        out_specs=[tile(shapes[k][0]) for k in kinds],
        out_shape=[jax.ShapeDtypeStruct((B, T, shapes[k][0]), shapes[k][1]) for k in kinds],
        compiler_params=_params(("parallel", "parallel")),
        name="norm_mod",
    )(x, g.reshape(1, D), mod)
    return outs[0] if len(kinds) == 1 else outs


def _route_kernel(t_ref, wt_ref, bias_ref, idx_ref, gate_ref, cnt_ref, run_ref):
    E = wt_ref.shape[0]
    tm = t_ref.shape[0]
    per = E // N_GROUPS

    @pl.when(pl.program_id(0) == 0)
    def _():
        run_ref[...] = jnp.zeros_like(run_ref)

    th, tl = _split2(t_ref[...])
    wh, wl = _split2(wt_ref[...])
    nt = lambda a, b: lax.dot_general(a, b, (((1,), (1,)), ((), ())), preferred_element_type=F32)
    logits = nt(wh, th) + (nt(wh, tl) + nt(wl, th))
    s = jax.nn.sigmoid(logits)
    sel = s + bias_ref[...]
    srow = [s[e:e + 1, :] for e in range(E)]
    row = [sel[e:e + 1, :] for e in range(E)]
    best_g, best_i = None, None
    for gq in range(N_GROUPS):
        a, b, c, d = row[gq * per:(gq + 1) * per]
        m1, n1, m2, n2 = jnp.maximum(a, b), jnp.minimum(a, b), jnp.maximum(c, d), jnp.minimum(c, d)
        score = jnp.maximum(m1, m2) + jnp.maximum(jnp.minimum(m1, m2), jnp.maximum(n1, n2))
        if gq == 0:
            best_g, best_i = score, jnp.zeros_like(score, dtype=jnp.int32)
        else:
            better = score > best_g
            best_g = jnp.where(better, score, best_g)
            best_i = jnp.where(better, gq, best_i)
    masked = [jnp.where(best_i == e // per, row[e], NEG_INF) for e in range(E)]

    def argmax_first(vals):
        top, idx = vals[0], jnp.zeros_like(best_i)
        for e in range(1, E):
            better = vals[e] > top
            top = jnp.where(better, vals[e], top)
            idx = jnp.where(better, e, idx)
        return idx

    i1 = argmax_first(masked)
    i2 = argmax_first([jnp.where(i1 == e, -jnp.inf, masked[e]) for e in range(E)])
    w1 = sum(jnp.where(i1 == e, srow[e], 0.0) for e in range(E))
    w2 = sum(jnp.where(i2 == e, srow[e], 0.0) for e in range(E))
    tot = w1 + w2
    chosen = jnp.concatenate([jnp.where((i1 == e) | (i2 == e), 1.0, 0.0) for e in range(E)], axis=0)
    earlier = lax.broadcasted_iota(jnp.int32, (tm, tm), 0) < lax.broadcasted_iota(jnp.int32, (tm, tm), 1)
    pos = run_ref[:, 0:1] + _bdot(chosen, earlier.astype(BF16))
    run_ref[...] = run_ref[...] + jnp.sum(chosen, axis=1, keepdims=True)
    p1 = sum(jnp.where(i1 == e, pos[e:e + 1, :], 0.0) for e in range(E)).astype(jnp.int32)
    p2 = sum(jnp.where(i2 == e, pos[e:e + 1, :], 0.0) for e in range(E)).astype(jnp.int32)
    zi = jnp.zeros((V7X_SUBLANES - 4, tm), jnp.int32)
    idx_ref[...] = jnp.concatenate([i1, i2, p1, p2, zi], axis=0)
    zf = jnp.zeros((V7X_SUBLANES - 2, tm), F32)
    gate_ref[...] = jnp.concatenate([w1 / tot, w2 / tot, zf], axis=0)
    cnt_ref[...] = run_ref[...]


def moe_route(t, router_w, router_bias, *, tm=512):
    n, D = t.shape
    E = router_w.shape[1]
    tm = min(tm, n)
    row8 = pl.BlockSpec((V7X_SUBLANES, tm), lambda i: (0, i))
    return pl.pallas_call(
        _route_kernel,
        grid=(n // tm,),
        in_specs=[pl.BlockSpec((tm, D), lambda i: (i, 0)),
                  pl.BlockSpec((E, D), lambda i: (0, 0)),
                  pl.BlockSpec((E, 1), lambda i: (0, 0))],
        out_specs=[row8, row8, pl.BlockSpec((E, V7X_LANES), lambda i: (0, 0))],
        out_shape=[jax.ShapeDtypeStruct((V7X_SUBLANES, n), jnp.int32),
                   jax.ShapeDtypeStruct((V7X_SUBLANES, n), F32),
                   jax.ShapeDtypeStruct((E, V7X_LANES), F32)],
        scratch_shapes=[pltpu.VMEM((E, V7X_LANES), F32)],
        compiler_params=_params(("arbitrary",)),
        name="moe_route",
    )(t, router_w.T, router_bias.reshape(E, 1))


MOE_TILE = 1152
COPY_ROWS = 256


def _dispatch_kernel(slot_ref, x_ref, xg_ref, sem):
    n = slot_ref.shape[0] // 2
    base = pl.program_id(0) * COPY_ROWS

    def copy(r, k):
        return pltpu.make_async_copy(x_ref.at[pl.ds(r, 1)], xg_ref.at[pl.ds(slot_ref[k * n + base + r], 1)], sem)

    def start(r, carry):
        copy(r, 0).start()
        copy(r, 1).start()
        return carry

    def wait(r, carry):
        copy(r, 0).wait()
        copy(r, 1).wait()
        return carry

    lax.fori_loop(0, COPY_ROWS, start, 0)
    lax.fori_loop(0, COPY_ROWS, wait, 0)


def moe_dispatch(slots, x_words, n_rows):
    n, W = x_words.shape
    return pl.pallas_call(
        _dispatch_kernel,
        grid_spec=pltpu.PrefetchScalarGridSpec(
            num_scalar_prefetch=1,
            grid=(n // COPY_ROWS,),
            in_specs=[pl.BlockSpec((COPY_ROWS, W), lambda i, s: (i, 0))],
            out_specs=pl.BlockSpec(memory_space=pl.ANY),
            scratch_shapes=[pltpu.SemaphoreType.DMA(())],
        ),
        out_shape=jax.ShapeDtypeStruct((n_rows, W), x_words.dtype),
        compiler_params=_params(("arbitrary",)),
        name="moe_dispatch",
    )(slots, x_words)


def _moe_up_kernel(te_ref, rows_ref, blk_ref, x_ref, wg_ref, wu_ref, o_ref, xs_ref):
    rows = rows_ref[pl.program_id(0)]
    half = x_ref.shape[1]

    @pl.when((rows > 0) & (pl.program_id(1) == 0))
    def _():
        valid = lax.broadcasted_iota(jnp.int32, x_ref.shape, 0) < rows
        lo, hi = _unpack_halves(jnp.where(valid, x_ref[...], jnp.uint32(0)))
        xs_ref[:, :half] = lo
        xs_ref[:, half:] = hi

    @pl.when(rows > 0)
    def _():
        x = xs_ref[...]
        hg = jnp.dot(x, wg_ref[...].astype(BF16), preferred_element_type=F32)
        hu = jnp.dot(x, wu_ref[...].astype(BF16), preferred_element_type=F32)
        o_ref[...] = (hg * jax.nn.sigmoid(hg) * hu).astype(o_ref.dtype)


def _moe_down_kernel(te_ref, rows_ref, blk_ref, a_ref, wd_ref, o_ref):
    @pl.when(rows_ref[pl.program_id(0)] > 0)
    def _():
        o_ref[...] = _bdot(a_ref[...], wd_ref[...])


def moe_experts(xg, w_gate, w_up, w_down, tile_expert, tile_rows, tile_block, *, layer, tm, tf=256, tn=1024):
    _, E, D, F = w_gate.shape
    nt = tile_expert.shape[0]
    sq = pl.Squeezed()
    nf, nc = F // tf, D // tn
    hold = lambda i, last, tr, j: jnp.where(tr[j] > 0, i, last)
    act = pl.pallas_call(
        _moe_up_kernel,
        grid_spec=pltpu.PrefetchScalarGridSpec(
            num_scalar_prefetch=3,
            grid=(nt, nf),
            in_specs=[pl.BlockSpec((tm, D // 2), lambda j, f, te, tr, tb: (tb[j], 0)),
                      pl.BlockSpec((sq, sq, D, tf), lambda j, f, te, tr, tb: (layer, te[j], 0, hold(f, nf - 1, tr, j))),
                      pl.BlockSpec((sq, sq, D, tf), lambda j, f, te, tr, tb: (layer, te[j], 0, hold(f, nf - 1, tr, j)))],
            out_specs=pl.BlockSpec((tm, tf), lambda j, f, te, tr, tb: (tb[j], hold(f, nf - 1, tr, j))),
            scratch_shapes=[pltpu.VMEM((tm, D), BF16)],
        ),
        out_shape=jax.ShapeDtypeStruct((nt * tm, F), BF16),
        compiler_params=_params(("arbitrary", "arbitrary")),
        name="moe_up",
    )(tile_expert, tile_rows, tile_block, xg, w_gate, w_up)
    return pl.pallas_call(
        _moe_down_kernel,
        grid_spec=pltpu.PrefetchScalarGridSpec(
            num_scalar_prefetch=3,
            grid=(nt, nc),
            in_specs=[pl.BlockSpec((tm, F), lambda j, c, te, tr, tb: (tb[j], 0)),
                      pl.BlockSpec((sq, sq, F, tn), lambda j, c, te, tr, tb: (layer, te[j], 0, hold(c, nc - 1, tr, j)))],
            out_specs=pl.BlockSpec((tm, tn), lambda j, c, te, tr, tb: (tb[j], hold(c, nc - 1, tr, j))),
        ),
        out_shape=jax.ShapeDtypeStruct((nt * tm, D), F32),
        compiler_params=_params(("arbitrary", "arbitrary")),
        name="moe_down",
    )(tile_expert, tile_rows, tile_block, act, w_down)


def _combine_kernel(slot_ref, y_ref, gate_ref, o_ref, ya_ref, yb_ref, sem):
    n = slot_ref.shape[0] // 2
    base = pl.program_id(0) * COPY_ROWS

    def copy(r, k):
        dst = ya_ref if k == 0 else yb_ref
        return pltpu.make_async_copy(y_ref.at[pl.ds(slot_ref[k * n + base + r], 1)], dst.at[pl.ds(r, 1)], sem)

    def start(r, carry):
        copy(r, 0).start()
        copy(r, 1).start()
        return carry

    def wait(r, carry):
        copy(r, 0).wait()
        copy(r, 1).wait()
        return carry

    lax.fori_loop(0, COPY_ROWS, start, 0)
    lax.fori_loop(0, COPY_ROWS, wait, 0)
    g = gate_ref[...].T
    o_ref[...] = ya_ref[...] * g[:, 0:1] + yb_ref[...] * g[:, 1:2]


def moe_combine(slots, y, gates):
    n = gates.shape[1]
    D = y.shape[1]
    return pl.pallas_call(
        _combine_kernel,
        grid_spec=pltpu.PrefetchScalarGridSpec(
            num_scalar_prefetch=1,
            grid=(n // COPY_ROWS,),
            in_specs=[pl.BlockSpec(memory_space=pl.ANY),
                      pl.BlockSpec((V7X_SUBLANES, COPY_ROWS), lambda i, s: (0, i))],
            out_specs=pl.BlockSpec((COPY_ROWS, D), lambda i, s: (i, 0)),
            scratch_shapes=[pltpu.VMEM((COPY_ROWS, D), F32), pltpu.VMEM((COPY_ROWS, D), F32),
                            pltpu.SemaphoreType.DMA(())],
        ),
        out_shape=jax.ShapeDtypeStruct((n, D), F32),
        compiler_params=_params(("arbitrary",)),
        name="moe_combine",
    )(slots, y, gates)


def _na_kernel(rpb_ref, q_ref, k_ref, v_ref, kc_ref, vc_ref, o_ref, tb_ref):
    h = pl.program_id(1)
    GW, HD = GRID_W, V7X_LANES
    n_rows = q_ref.shape[1] // GW
    wr = NA_WIN_R
    n_dc = 2 * NA_WIN_C - 1
    qcol = lax.broadcasted_iota(jnp.int32, (GW, 2 * GW), 0)
    lane = lax.broadcasted_iota(jnp.int32, (GW, 2 * GW), 1)
    kcol = lane & (GW - 1)
    upper = lane >= GW
    dci = jnp.clip(kcol - qcol + NA_WIN_C - 1, 0, n_dc - 1)
    cstart = jnp.clip(qcol - NA_WIN_C // 2, 0, GW - NA_WIN_C)
    col_ok = (kcol >= cstart) & (kcol < cstart + NA_WIN_C)
    for dr in range(2 * wr - 2):
        acc = jnp.zeros((GW, 2 * GW), F32)
        for j in range(n_dc):
            lo = rpb_ref[h, dr * n_dc + j]
            hi = rpb_ref[h, (dr + 1) * n_dc + j]
            acc = jnp.where(dci == j, jnp.where(upper, hi, lo), acc)
        tb_ref[dr] = acc
    ok4 = jnp.concatenate([col_ok] * (wr // 2), axis=1)
    kc = kc_ref[0]
    vc = vc_ref[0]
    scale = HD ** -0.5

    def body(i, carry):
        rr = [i * NA_LOCK + u for u in range(NA_LOCK)]
        rs = [jnp.clip(r - wr // 2, 0, n_rows - wr) for r in rr]
        qrow = [pl.ds(pl.multiple_of(r * GW, GW), GW) for r in rr]
        krow = [pl.ds(pl.multiple_of(s * GW, GW), wr * GW) for s in rs]
        q = [q_ref[0, qr, :] * scale for qr in qrow]
        s_lat = [_bdot_nt(a, k_ref[0, kr, :]) for a, kr in zip(q, krow)]
        s_ctx = [_bdot_nt(a, kc) for a in q]
        bias = [jnp.concatenate([tb_ref[s - r + wr - 1 + 2 * w] for w in range(wr // 2)], axis=1)
                for r, s in zip(rr, rs)]
        s_lat = [jnp.where(ok4, a + b, NEG_INF) for a, b in zip(s_lat, bias)]
        m = [jnp.maximum(jnp.max(a, axis=-1, keepdims=True), jnp.max(b, axis=-1, keepdims=True))
             for a, b in zip(s_lat, s_ctx)]
        p_lat = [jnp.exp(a - c) for a, c in zip(s_lat, m)]
        p_ctx = [jnp.exp(a - c) for a, c in zip(s_ctx, m)]
        den = [jnp.sum(a, axis=-1, keepdims=True) + jnp.sum(b, axis=-1, keepdims=True)
               for a, b in zip(p_lat, p_ctx)]
        o_lat = [_bdot(a, v_ref[0, kr, :]) for a, kr in zip(p_lat, krow)]
        o_ctx = [_bdot(a, vc) for a in p_ctx]
        for qr, a, b, d in zip(qrow, o_lat, o_ctx, den):
            o_ref[0, qr, :] = (a + b) / d
        return carry

    lax.fori_loop(0, n_rows // NA_LOCK, body, 0)


def neighbourhood_attention(pl_lat, kv_ctx, rpb):
    B, S, _ = pl_lat.shape
    Lc = kv_ctx.shape[1]
    H, HD = NA_HEADS, V7X_LANES
    lat = lambda off: pl.BlockSpec((1, S, HD), lambda b, h: (b, 0, off + h))
    cx = lambda off: pl.BlockSpec((1, Lc, HD), lambda b, h: (b, 0, off + h))
    return pl.pallas_call(
        _na_kernel,
        grid=(B, H),
        in_specs=[pl.BlockSpec(memory_space=pltpu.SMEM), lat(0), lat(H), lat(2 * H), cx(0), cx(H)],
        out_specs=pl.BlockSpec((1, S, HD), lambda b, h: (b, 0, h)),
        out_shape=jax.ShapeDtypeStruct((B, S, H * HD), F32),
        scratch_shapes=[pltpu.VMEM((2 * NA_WIN_R - 2, GRID_W, 2 * GRID_W), F32)],
        compiler_params=_params(("parallel", "parallel")),
        name="neighbourhood_attention",
    )(rpb.reshape(H, -1), pl_lat, pl_lat, pl_lat, kv_ctx, kv_ctx)


DFT_N1 = 64
DFT_N2 = 128
DFT_SUB = 8
DFT_LOCK = 4


@functools.lru_cache(maxsize=None)
def _dft_constants():
    n1, n2 = DFT_N1, DFT_N2
    n = n1 * n2
    a = np.arange(n1)
    f1 = np.exp(-2j * np.pi * np.outer(a, a) / n1)
    half = f1[:, :n1 // 2]
    eye = np.eye(DFT_SUB)
    fwd = np.kron(half, eye)
    inv = np.kron(np.conj(f1)[:n1 // 2, :], eye) / n
    b = np.arange(n2)
    f2 = np.exp(-2j * np.pi * np.outer(b, b) / n2)
    tw = np.exp(-2j * np.pi * np.outer(a, b) / n)
    m = f2[None, :, :] * tw[:, None, :]
    mt = np.transpose(m, (0, 2, 1))
    c = lambda z: (np.ascontiguousarray(z.real, dtype=np.float32), np.ascontiguousarray(z.imag, dtype=np.float32))
    return c(fwd) + c(inv) + c(m) + c(mt)


def _hyena_conv_kernel(hf_ref, hb_ref, z_ref, fwr_ref, fwi_ref, ivr_ref, ivi_ref,
                       mr_ref, mi_ref, mtr_ref, mti_ref, o_ref, ar_ref, ai_ref, kr_ref, ki_ref):
    L, TC = hf_ref.shape
    N1, N2, SUB = DFT_N1, DFT_N2, DFT_SUB
    nblk = N2 // SUB

    def stage1(load):
        def blk(nb, carry):
            off = pl.multiple_of(nb * SUB, SUB)
            xs = jnp.concatenate([load(n1 * N2 + off) for n1 in range(N1 // 2)], axis=0)
            yr = _bdot(fwr_ref[...], xs)
            yi = _bdot(fwi_ref[...], xs)
            for k1 in range(N1):
                ar_ref[pl.ds(k1 * N2 + off, SUB), :] = yr[k1 * SUB:(k1 + 1) * SUB]
                ai_ref[pl.ds(k1 * N2 + off, SUB), :] = yi[k1 * SUB:(k1 + 1) * SUB]
            return carry

        lax.fori_loop(0, nblk, blk, 0)

    def cmul_many(m_r, m_i, x_r, x_i, conj):
        rr = [_bdot(a, b) for a, b in zip(m_r, x_r)]
        ii = [_bdot(a, b) for a, b in zip(m_i, x_i)]
        ri = [_bdot(a, b) for a, b in zip(m_r, x_i)]
        ir = [_bdot(a, b) for a, b in zip(m_i, x_r)]
        if conj:
            return [a + b for a, b in zip(rr, ii)], [a - b for a, b in zip(ri, ir)]
        return [a - b for a, b in zip(rr, ii)], [a + b for a, b in zip(ri, ir)]

    def spectrum(i):
        k1s = [i * DFT_LOCK + u for u in range(DFT_LOCK)]
        rows = [pl.ds(pl.multiple_of(k1 * N2, N2), N2) for k1 in k1s]
        x_r, x_i = cmul_many([mr_ref[k1] for k1 in k1s], [mi_ref[k1] for k1 in k1s],
                             [ar_ref[r, :] for r in rows], [ai_ref[r, :] for r in rows], conj=False)
        return k1s, rows, x_r, x_i

    stage1(lambda r0: hf_ref[pl.ds(r0, SUB), :])

    def filt_f(i, carry):
        _, rows, x_r, x_i = spectrum(i)
        for r, a, b in zip(rows, x_r, x_i):
            kr_ref[r, :] = a
            ki_ref[r, :] = b
        return carry

    lax.fori_loop(0, N1 // DFT_LOCK, filt_f, 0)
    sub_iota = lax.broadcasted_iota(jnp.int32, (SUB, TC), 0)
    stage1(lambda r0: jnp.where(sub_iota + r0 == 0, 0.0, hb_ref[pl.ds(r0, SUB), :]))

    def filt_b(i, carry):
        _, rows, x_r, x_i = spectrum(i)
        for r, a, b in zip(rows, x_r, x_i):
            kr_ref[r, :] += a
            ki_ref[r, :] -= b
        return carry

    lax.fori_loop(0, N1 // DFT_LOCK, filt_b, 0)

    for b in range(z_ref.shape[0]):
        stage1(lambda r0: z_ref[b, pl.ds(r0, SUB), :])

        def conv(i, carry):
            k1s, rows, x_r, x_i = spectrum(i)
            f_r, f_i = [kr_ref[r, :] for r in rows], [ki_ref[r, :] for r in rows]
            y_r = [a * c - b_ * d for a, b_, c, d in zip(x_r, x_i, f_r, f_i)]
            y_i = [a * d + b_ * c for a, b_, c, d in zip(x_r, x_i, f_r, f_i)]
            o_r, o_i = cmul_many([mtr_ref[k1] for k1 in k1s], [mti_ref[k1] for k1 in k1s], y_r, y_i, conj=True)
            for r, a, c in zip(rows, o_r, o_i):
                ar_ref[r, :] = a
                ai_ref[r, :] = c
            return carry

        lax.fori_loop(0, N1 // DFT_LOCK, conv, 0)
        def last(nb, carry):
            off = pl.multiple_of(nb * SUB, SUB)
            br = jnp.concatenate([ar_ref[pl.ds(k1 * N2 + off, SUB), :] for k1 in range(N1)], axis=0)
            bi = jnp.concatenate([ai_ref[pl.ds(k1 * N2 + off, SUB), :] for k1 in range(N1)], axis=0)
            y = _bdot(ivr_ref[...], br) - _bdot(ivi_ref[...], bi)
            for n1 in range(N1 // 2):
                o_ref[b, pl.ds(n1 * N2 + off, SUB), :] = y[n1 * SUB:(n1 + 1) * SUB]
            return carry

        lax.fori_loop(0, nblk, last, 0)


def hyena_long_conv(z, h_f, h_b, *, tc=V7X_LANES):
    B, L, C = z.shape
    assert L == DFT_N1 * DFT_N2 // 2
    consts = [jnp.asarray(a, BF16) for a in _dft_constants()]
    whole = lambda a: pl.BlockSpec(a.shape, lambda j: (0,) * a.ndim)
    col = pl.BlockSpec((L, tc), lambda j: (0, j))
    bcol = pl.BlockSpec((B, L, tc), lambda j: (0, 0, j))
    return pl.pallas_call(
        _hyena_conv_kernel,
        grid=(C // tc,),
        in_specs=[col, col, bcol] + [whole(a) for a in consts],
        out_specs=bcol,
        out_shape=jax.ShapeDtypeStruct((B, L, C), F32),
        scratch_shapes=[pltpu.VMEM((2 * L, tc), F32)] * 4,
        compiler_params=_params(("parallel",)),
        name="hyena_long_conv",
    )(h_f, h_b, z, *consts)


def _pick(n, options):
    for t in options:
        if n % t == 0:
            return t
    raise ValueError(f"no tile for {n} in {options}")


def _mm(a, w, **kw):
    tm = _pick(a.shape[0], (1088, 1024, 768, 512, 256, 8))
    tn = _pick(w.shape[1], (512, 640, 384, 256, 128))
    return matmul(a, w, tm=tm, tn=tn, **kw)


def _pad_to(a, axis, mult):
    pad = (-a.shape[axis]) % mult
    if pad == 0:
        return a
    widths = [(0, 0)] * a.ndim
    widths[axis] = (0, pad)
    return jnp.pad(a, widths)


def _silu(x):
    return x * jax.nn.sigmoid(x)


def _seq_parts(x, lc, fn):
    if lc == 0:
        return fn(x)
    return jnp.concatenate([fn(x[:, :lc]), fn(x[:, lc:])], axis=1)


def _dwconv(x, w):
    k, length = w.shape[0], x.shape[1]
    xp = jnp.pad(x, ((0, 0), (k // 2, k // 2), (0, 0)))
    return sum(xp[:, j:j + length] * w[j] for j in range(k))


def _centred_shift(p):
    prev = jnp.pad(p, ((0, 0), (1, 0), (0, 0)))[:, :-1]
    nxt = jnp.pad(p, ((0, 0), (0, 1), (0, 0)))[:, 1:]
    return 0.5 * (prev + nxt)


def _l2norm(t):
    return t * lax.rsqrt(jnp.sum(t * t, -1, keepdims=True) + 1e-6)


def _axial_rope(t):
    length, d = t.shape[1], t.shape[-1]
    nf = d // 4
    pos = jnp.arange(length)
    inv = ROPE_BASE ** (-jnp.arange(nf, dtype=F32) / nf)

    def cs(p):
        ang = p.astype(F32)[:, None] * inv[None, :]
        return jnp.cos(ang), jnp.sin(ang)

    cr, sr = cs(pos // GRID_W)
    cc, sc = cs(pos % GRID_W)
    cos = jnp.concatenate([cr, cr, cc, cc], -1)[None, :, None, :]
    sin = jnp.concatenate([-sr, sr, -sc, sc], -1)[None, :, None, :]
    swapped = jnp.flip(t.reshape(*t.shape[:-1], 2, 2, nf), axis=-2).reshape(t.shape)
    return t * cos + swapped * sin


def _mixer_ab(h, lc, w_in, dn_conv_w, dn_a_log, dn_dt_bias, dn_norm_w,
              rw_mu, rw_w0, rw_w2, rw_a0, rw_a2, rw_g2, rw_k_k, rw_k_a, rw_r_k, rw_ln_w, rw_ln_b, w_out):
    B, T, D = h.shape
    mw = D // 2
    hd_dn = mw // DN_HEADS
    h_rw = mw // RW_HD
    p_a = 4 * mw + 4 * DN_HEADS
    h2 = h.reshape(B * T, D)
    pa = _mm(h2, _pad_to(w_in[:, :p_a], 1, V7X_LANES).astype(BF16)).reshape(B, T, -1)
    pb = _mm(h2, _pad_to(w_in[:, p_a:], 1, V7X_LANES).astype(BF16)).reshape(B, T, -1)

    qkv = _seq_parts(pa[..., :3 * mw], lc, lambda u: _silu(_dwconv(u, dn_conv_w)))
    q, k, v = (t.reshape(B, T, DN_HEADS, hd_dn) for t in jnp.split(qkv, 3, axis=-1))
    rope_lat = lambda t: jnp.concatenate([t[:, :lc], _axial_rope(t[:, lc:])], axis=1)
    q, k = rope_lat(_l2norm(q)), rope_lat(_l2norm(k))
    q = q * hd_dn ** -0.5
    z = pa[..., 3 * mw:4 * mw]
    bg = pa[..., 4 * mw:p_a].reshape(B, T, 4, DN_HEADS)
    beta = jax.nn.sigmoid(bg[:, :, :2])
    g = -jnp.exp(dn_a_log) * jax.nn.softplus(bg[:, :, 2:] + dn_dt_bias)
    flat = lambda t: t.reshape(B, T, mw)
    o = 0.0
    for d in range(2):
        bk = k * beta[:, :, d, :, None]
        gd = jnp.broadcast_to(g[:, :, d, :, None], k.shape)
        o = o + delta_scan(flat(q), flat(k), None, flat(bk), flat(v), flat(gd),
                           head_dim=hd_dn, reverse=d == 1, scalar_decay=True, beta_excl=False)
    o = o.reshape(B, T, DN_HEADS, hd_dn)
    o = o * lax.rsqrt(jnp.mean(o * o, -1, keepdims=True) + NORM_EPS) * dn_norm_w
    dn = flat(o * _silu(z).reshape(o.shape))

    p_b = w_in.shape[1] - p_a
    p = pb[..., :p_b]
    p = p + (_seq_parts(p, lc, _centred_shift) - p) * rw_mu
    r, k, v = p[..., :mw], p[..., mw:2 * mw], p[..., 2 * mw:3 * mw]
    o1 = 3 * mw
    n_w, n_a = rw_w2.shape[1], rw_a2.shape[1]
    o2 = o1 + 2 * n_w
    o3 = o2 + 2 * n_a
    wl = p[..., o1:o2].reshape(B * T, 2, n_w)
    al = p[..., o2:o3].reshape(B * T, 2, n_a)
    gl = p[..., o3:].reshape(B * T, -1)
    gate = _mm(_pad_to(jax.nn.sigmoid(gl), 1, V7X_LANES), _pad_to(rw_g2, 0, V7X_LANES)).reshape(B, T, mw)
    heads = lambda t: t.reshape(B, T, h_rw, RW_HD)
    kk = flat(_l2norm(heads(k * rw_k_k)))
    y = 0.0
    for d in range(2):
        w = -jax.nn.softplus(-(rw_w0[d] + _mm(jnp.tanh(wl[:, d]), rw_w2[d]).reshape(B, T, mw))) - 0.5
        a = jax.nn.sigmoid(rw_a0[d] + _mm(al[:, d], rw_a2[d]).reshape(B, T, mw))
        kd = k * (1.0 + (a - 1.0) * rw_k_a)
        y = y + delta_scan(r, kk, -(kk * a), kd, v, -jnp.exp(w),
                           head_dim=RW_HD, reverse=d == 1, scalar_decay=False, beta_excl=True)
    y = heads(y)
    mu = jnp.mean(y, -1, keepdims=True)
    var = jnp.mean(jnp.square(y - mu), -1, keepdims=True)
    yn = flat((y - mu) * lax.rsqrt(var + RW_LN_EPS)) * rw_ln_w + rw_ln_b
    bonus = flat(jnp.sum(heads(r) * heads(k) * rw_r_k, -1, keepdims=True) * heads(v))
    rw = (yn + bonus) * gate

    cat = jnp.concatenate([dn, rw], -1).astype(BF16).reshape(B * T, D)
    return _mm(cat, w_out.astype(BF16)).reshape(B, T, D)


def _hyena_filters(length, ch, w1, b1, freq, w2, b2, w3, b3, w4):
    t = jnp.linspace(0.0, 1.0, length, dtype=F32)
    bands = (HY_EMB - 1) // 2
    wpos = 2 * math.pi * jnp.arange(length, dtype=F32) / length
    fb = jnp.linspace(1e-4, bands - 1, bands, dtype=F32)
    ang = wpos[:, None] * fb[None, :]
    z = jnp.concatenate([t[:, None], jnp.cos(ang), -jnp.sin(ang)], axis=-1)
    dense = lambda u, w: _mm(_pad_to(u, 1, V7X_LANES), _pad_to(_pad_to(w, 0, V7X_LANES), 1, V7X_LANES),
                             passes=3)[:, :w.shape[1]]
    h = jnp.sin(freq * (dense(z, w1) + b1))
    h = jnp.sin(freq * (dense(h, w2) + b2))
    h = jnp.sin(freq * (dense(h, w3) + b3))
    h = dense(h, w4).reshape(length, 2, ch)
    deltas = jnp.abs(jnp.linspace(HY_MIN_DECAY, HY_MAX_DECAY, ch, dtype=F32))
    window = jnp.exp(-t[:, None] * deltas[None, :])
    return h[:, 0] * window, h[:, 1] * window


def _mixer_cd(hc, hl, w_in, na_rpb, hy_conv_w, hy_conv_b, filt, hy_skip, w_out):
    B, S, D = hl.shape
    lc = hc.shape[1]
    mw = D // 2
    w_bf = w_in.astype(BF16)
    pl_ = _mm(hl.reshape(B * S, D), w_bf).reshape(B, S, -1)
    kv_c = _mm(hc.reshape(B * lc, D), w_bf[:, mw:3 * mw]).reshape(B, lc, 2 * mw)
    att = neighbourhood_attention(pl_, kv_c, na_rpb)
    u = _dwconv(pl_[..., 3 * mw:], hy_conv_w) + hy_conv_b
    x0, x1, v = jnp.split(u, 3, axis=-1)
    h_f, h_b = _hyena_filters(S, mw, *filt)
    z = x1 * v
    hy = x0 * (hyena_long_conv(z, h_f, h_b) + z * hy_skip)
    cat = jnp.concatenate([att, hy], -1).astype(BF16).reshape(B * S, D)
    return _mm(cat, w_out.astype(BF16)).reshape(B, S, D)


def _moe(t_packed, t_f32, router_w, router_bias, w_gate, w_up, w_down, layer, tile=MOE_TILE):
    n, D = t_f32.shape
    E = w_gate.shape[1]
    idx, gates, counts = moe_route(t_f32, router_w, router_bias)
    cnt = counts[:, 0].astype(jnp.int32)
    nt = -(-2 * n // tile) + E
    tiles = (cnt + tile - 1) // tile
    ends = jnp.cumsum(tiles)
    starts = ends - tiles
    used = ends[-1]
    j = jnp.arange(nt, dtype=jnp.int32)
    jj = jnp.minimum(j, used - 1)
    tile_expert = jnp.sum((jj[:, None] >= ends[None, :]).astype(jnp.int32), axis=1)
    tile_rows = jnp.where(j < used, jnp.clip(cnt[tile_expert] - (jj - starts[tile_expert]) * tile, 0, tile), 0)
    row0 = starts * tile
    slots = jnp.concatenate([row0[idx[0]] + idx[2], row0[idx[1]] + idx[3]]).astype(jnp.int32)
    xg = moe_dispatch(slots, t_packed, nt * tile)
    y = moe_experts(xg, w_gate, w_up, w_down, tile_expert.astype(jnp.int32), tile_rows.astype(jnp.int32), jj,
                    layer=layer, tm=tile)
    return moe_combine(slots, y, gates)


def kernel(x, c, ctx, c_ctx, ada_w, ada_b, norm1_g, norm2_g, final_g, ab_w_in, dn_conv_w, dn_a_log, dn_dt_bias, dn_norm_w, rw_mu, rw_w0, rw_w2, rw_a0, rw_a2, rw_g2, rw_k_k, rw_k_a, rw_r_k, rw_ln_w, rw_ln_b, ab_w_out, cd_w_in, na_rpb, hy_conv_w, hy_conv_b, hy_w1, hy_b1, hy_freq, hy_w2, hy_b2, hy_w3, hy_b3, hy_w4, hy_skip, cd_w_out, router_w, router_bias, moe_w_gate, moe_w_up, moe_w_down):
    B, S, D = x.shape
    lc = ctx.shape[1]
    assert ada_w.shape[0] == 2 and lc == ROW_TILE and S % ROW_TILE == 0 and B + 1 <= V7X_SUBLANES
    T = lc + S
    ctx_tiles = lc // ROW_TILE
    cond = jnp.concatenate([c, c_ctx[None], jnp.zeros((V7X_SUBLANES - B - 1, D), F32)], axis=0)
    mods = [matmul(cond, ada_w, tm=V7X_SUBLANES, tn=512, bias=ada_b[l], a_silu=True, layer=l,
                   name="adaln")[:B + 1].reshape(B + 1, 6, D) for l in range(2)]

    def rows(m, idx):
        return jnp.concatenate([jnp.broadcast_to(m[B, idx], (B, lc, D)),
                                jnp.broadcast_to(m[:B, idx][:, None, :], (B, S, D))], axis=1)

    m = mods[0]
    xa = jnp.concatenate([ctx, x], axis=1)
    h = norm_mod(xa, norm1_g[0], m, which=0, ctx_tiles=ctx_tiles)
    mix = _mixer_ab(h, lc, ab_w_in[0], dn_conv_w[0], dn_a_log[0], dn_dt_bias[0], dn_norm_w[0],
                    rw_mu[0], rw_w0[0], rw_w2[0], rw_a0[0], rw_a2[0], rw_g2[0], rw_k_k[0], rw_k_a[0],
                    rw_r_k[0], rw_ln_w[0], rw_ln_b[0], ab_w_out[0])
    xa = xa + rows(m, 2) * mix
    h_pk, h_f32 = norm_mod(xa, norm2_g[0], m, which=1, ctx_tiles=ctx_tiles, kinds=("packed", "f32"))
    f = _moe(h_pk.reshape(B * T, D // 2), h_f32.reshape(B * T, D), router_w, router_bias,
             moe_w_gate, moe_w_up, moe_w_down, 0).reshape(B, T, D)
    xa = xa + rows(m, 5) * f

    m = mods[1]
    cx, xl = xa[:, :lc], xa[:, lc:]
    hl = norm_mod(xl, norm1_g[1], m, which=0, ctx_tiles=0)
    hc = norm_mod(cx, norm1_g[1], m, which=0, ctx_tiles=ctx_tiles)
    filt = (hy_w1[0], hy_b1[0], hy_freq[0], hy_w2[0], hy_b2[0], hy_w3[0], hy_b3[0], hy_w4[0])
    ml = _mixer_cd(hc, hl, cd_w_in[0], na_rpb[0], hy_conv_w[0], hy_conv_b[0], filt, hy_skip[0], cd_w_out[0])
    xl = xl + m[:B, 2][:, None, :] * ml
    h_pk, h_f32 = norm_mod(xl, norm2_g[1], m, which=1, ctx_tiles=0, kinds=("packed", "f32"))
    f = _moe(h_pk.reshape(B * S, D // 2), h_f32.reshape(B * S, D), router_w, router_bias,
             moe_w_gate, moe_w_up, moe_w_down, 1).reshape(B, S, D)
    xl = xl + m[:B, 5][:, None, :] * f
    return xl * lax.rsqrt(jnp.mean(xl * xl, -1, keepdims=True) + NORM_EPS) * final_g
```

```python
import functools
import math

import jax
import jax.numpy as jnp
import numpy as np
from jax import lax
from jax.experimental import pallas as pl
from jax.experimental.pallas import tpu as pltpu

F32 = jnp.float32
BF16 = jnp.bfloat16

V7X_LANES = 128
V7X_SUBLANES = 8
V7X_MXU_DIM = 256
V7X_VMEM_LIMIT_BYTES = 56 * 1024 * 1024

GRID_W = 64
NORM_EPS = 1e-6
NEG_INF = -1e30
ROPE_BASE = 10000.0
DN_HEADS = 16
DN_CONV = 5
RW_HD = 64
RW_LN_EPS = 64e-5
NA_HEADS = 16
NA_WIN_R = 8
NA_WIN_C = 16
NA_LOCK = 4
HY_EMB = 33
HY_TARGET = 1e-2
HY_MAX_DECAY = math.log(HY_TARGET) / 0.3
HY_MIN_DECAY = math.log(HY_TARGET) / 1.5
N_EXPERTS = 16
N_GROUPS = 4

ROW_TILE = 256
SCAN_CHUNK = 64


def _params(semantics, vmem=V7X_VMEM_LIMIT_BYTES):
    return pltpu.CompilerParams(dimension_semantics=semantics, vmem_limit_bytes=vmem)


def _bdot(a, b):
    return jnp.dot(a.astype(BF16), b.astype(BF16), preferred_element_type=F32)


def _bdot_nt(a, b):
    return lax.dot_general(a.astype(BF16), b.astype(BF16), (((1,), (1,)), ((), ())),
                           preferred_element_type=F32)


def _split2(x):
    hi = x.astype(BF16)
    lo = (x - hi.astype(F32)).astype(BF16)
    return hi, lo


def _split3(x):
    hi = x.astype(BF16)
    r1 = x - hi.astype(F32)
    mid = r1.astype(BF16)
    lo = (r1 - mid.astype(F32)).astype(BF16)
    return hi, mid, lo


def _dot3(a, b):
    ah, al = _split2(a)
    bh, bl = _split2(b)
    d = lambda u, v: jnp.dot(u, v, preferred_element_type=F32)
    return d(ah, bh) + (d(ah, bl) + d(al, bh))


def _dot_exact_left(m, x):
    hi, mid, lo = _split3(x)
    d = lambda v: jnp.dot(m, v, preferred_element_type=F32)
    return d(hi) + (d(mid) + d(lo))


def _unit_tri_inverse(mats, row, col):
    eye = (row == col).astype(F32)
    in8 = (row >> 3) == (col >> 3)
    a8 = [jnp.where(in8, a, 0.0) for a in mats]
    a2 = [_bdot(u, u) for u in a8]
    p = [eye + u for u in a8]
    p = [u + _bdot(u, w) for u, w in zip(p, a2)]
    a4 = [_bdot(u, u) for u in a2]
    x = [u + _bdot(u, w) for u, w in zip(p, a4)]
    for sh in (3, 4, 5):
        m = ((row >> (sh + 1)) == (col >> (sh + 1))) & ((row >> sh) != (col >> sh))
        left = [_bdot(u, jnp.where(m, a, 0.0)) for u, a in zip(x, mats)]
        x = [u + _bdot(w, u) for u, w in zip(x, left)]
    return x


def _delta_scan_kernel(*refs, lane_groups, head_dim, reverse, scalar_decay, beta_excl, ka_is_neg_kk):
    if ka_is_neg_kk:
        r_ref, kb_ref, kk_ref, v_ref, g_ref, y_ref, t_ref = refs
        ka_ref = None
    else:
        r_ref, kb_ref, ka_ref, kk_ref, v_ref, g_ref, y_ref, t_ref = refs
    TT, W, C = ROW_TILE, V7X_MXU_DIM, SCAN_CHUNK
    n_heads = W // head_dim
    hshift = int(math.log2(head_dim))
    groups = range(lane_groups)
    units = [(gs, j) for gs in groups for j in range(n_heads)]

    @pl.when(pl.program_id(2) == 0)
    def _():
        t_ref[...] = jnp.zeros_like(t_ref)

    row = lax.broadcasted_iota(jnp.int32, (TT, TT), 0)
    col = lax.broadcasted_iota(jnp.int32, (TT, TT), 1)
    same = (row >> 6) == (col >> 6)
    if reverse:
        strict, incl = same & (col > row), same & (col >= row)
    else:
        strict, incl = same & (col < row), same & (col <= row)
    incl_bf, same_bf = incl.astype(BF16), same.astype(BF16)
    load = lambda ref: [ref[0, :, gs * W:(gs + 1) * W] for gs in groups]

    g = load(g_ref)
    gi = [_dot_exact_left(incl_bf, u) for u in g]
    gc = [_dot_exact_left(same_bf, u) for u in g]
    gb = [u - w for u, w in zip(gi, g)] if beta_excl else gi
    r, kb, kk, v = load(r_ref), load(kb_ref), load(kk_ref), load(v_ref)
    ka = [-u for u in kk] if ka_is_neg_kk else load(ka_ref)
    e_rest = [jnp.exp(c - i) for c, i in zip(gc, gi)]
    r_s = [u * jnp.exp(i) for u, i in zip(r, gi)]
    kb_s = [u * jnp.exp(i) for u, i in zip(kb, gb)]
    ka_e = [u * e for u, e in zip(ka, e_rest)]
    kk_e = [u * e for u, e in zip(kk, e_rest)]
    if scalar_decay:
        assert head_dim == V7X_LANES
        gram_r, gram_b, gram_k = r, kb, kk
        gram_a = None if ka_is_neg_kk else ka
    else:
        inv = [jnp.exp(-i) for i in gi]
        gram_r, gram_b = r_s, kb_s
        gram_k = [u * w for u, w in zip(kk, inv)]
        gram_a = None if ka_is_neg_kk else [u * w for u, w in zip(ka, inv)]

    lane_head = lax.broadcasted_iota(jnp.int32, (TT, W), 1) >> hshift
    mj = [lane_head == j for j in range(n_heads)]
    bl = [jnp.where(mj[j], gram_b[gs], 0.0) for gs, j in units]
    rl = [jnp.where(mj[j], gram_r[gs], 0.0) for gs, j in units]
    a_bk = [_bdot_nt(bl[u], gram_k[gs]) for u, (gs, j) in enumerate(units)]
    a_rk = [_bdot_nt(rl[u], gram_k[gs]) for u, (gs, j) in enumerate(units)]
    if ka_is_neg_kk:
        a_ba, a_ra = [-u for u in a_bk], [-u for u in a_rk]
    else:
        a_ba = [_bdot_nt(bl[u], gram_a[gs]) for u, (gs, j) in enumerate(units)]
        a_ra = [_bdot_nt(rl[u], gram_a[gs]) for u, (gs, j) in enumerate(units)]
    if scalar_decay:
        for u, (gs, j) in enumerate(units):
            sl = slice(j * head_dim, (j + 1) * head_dim)
            gi_m = jnp.concatenate([gi[gs][:, sl]] * n_heads, axis=1)
            gb_m = jnp.concatenate([gb[gs][:, sl]] * n_heads, axis=1)
            gi_t = gi_m.T
            d_b = jnp.exp(jnp.minimum(gb_m - gi_t, 0.0))
            d_r = jnp.exp(jnp.minimum(gi_m - gi_t, 0.0))
            a_ba[u], a_bk[u], a_ra[u], a_rk[u] = a_ba[u] * d_b, a_bk[u] * d_b, a_ra[u] * d_r, a_rk[u] * d_r
    a_ba = [jnp.where(strict, u, 0.0) for u in a_ba]
    a_bk = [jnp.where(strict, u, 0.0) for u in a_bk]
    a_ra = [jnp.where(incl, u, 0.0) for u in a_ra]
    a_rk = [jnp.where(incl, u, 0.0) for u in a_rk]
    x = _unit_tri_inverse(a_ba, row, col)
    bj = [_bdot(x[u], kb_s[gs]) for u, (gs, j) in enumerate(units)]
    t1 = [_bdot(a_bk[u], v[gs]) for u, (gs, j) in enumerate(units)]
    wj = [_bdot(x[u], t1[u]) for u in range(len(units))]
    ra = [_bdot(a_ra[u], bj[u]) for u in range(len(units))]
    ya = [_bdot(a_ra[u], wj[u]) for u in range(len(units))]
    yb = [_bdot(a_rk[u], v[gs]) for u, (gs, j) in enumerate(units)]
    b_hat, w1, r_hat, y0 = [], [], [], []
    for gs in groups:
        bh = wh = yh = jnp.zeros((TT, W), F32)
        rh = r_s[gs]
        for j in range(n_heads):
            u = gs * n_heads + j
            bh = jnp.where(mj[j], bj[u], bh)
            wh = jnp.where(mj[j], wj[u], wh)
            rh = rh + jnp.where(mj[j], ra[u], 0.0)
            yh = jnp.where(mj[j], ya[u] + yb[u], yh)
        b_hat.append(bh)
        w1.append(wh)
        r_hat.append(rh)
        y0.append(yh)

    rw = lax.broadcasted_iota(jnp.int32, (W, W), 0)
    cw = lax.broadcasted_iota(jnp.int32, (W, W), 1)
    blockdiag = (rw >> hshift) == (cw >> hshift)
    eye_w = rw == cw
    t = [t_ref[gs] for gs in groups]
    for c in (range(TT // C - 1, -1, -1) if reverse else range(TT // C)):
        sl = slice(c * C, (c + 1) * C)
        for gs in groups:
            y_ref[0, sl, gs * W:(gs + 1) * W] = _bdot(r_hat[gs][sl], t[gs]) + y0[gs][sl]
        ka_t = [ka_e[gs][sl].T for gs in groups]
        kk_t = [kk_e[gs][sl].T for gs in groups]
        decay_col = [jnp.sum(jnp.where(eye_w, jnp.exp(gc[gs][c * C:c * C + 1, :]), 0.0), axis=1, keepdims=True)
                     for gs in groups]
        mix = [jnp.where(blockdiag, _bdot(ka_t[gs], b_hat[gs][sl]), 0.0) for gs in groups]
        add = [jnp.where(blockdiag, _bdot(ka_t[gs], w1[gs][sl]) + _bdot(kk_t[gs], v[gs][sl]), 0.0) for gs in groups]
        t = [decay_col[gs] * t[gs] + _bdot(mix[gs], t[gs]) + add[gs] for gs in groups]
    for gs in groups:
        t_ref[gs] = t[gs]


def delta_scan(r, kb, ka, kk, v, g, *, head_dim, reverse, scalar_decay, beta_excl, lane_groups=2):
    B, T, HW = r.shape
    n_t = T // ROW_TILE
    width = lane_groups * V7X_MXU_DIM
    if reverse:
        tile = lambda i: jnp.where(i == 0, 0, n_t - i)
    else:
        tile = lambda i: i
    spec = pl.BlockSpec((1, ROW_TILE, width), lambda b, h, i: (b, tile(i), h))
    kern = functools.partial(_delta_scan_kernel, lane_groups=lane_groups, head_dim=head_dim, reverse=reverse,
                             scalar_decay=scalar_decay, beta_excl=beta_excl, ka_is_neg_kk=ka is None)
    args = [a for a in (r, kb, ka, kk, v, g) if a is not None]
    return pl.pallas_call(
        kern,
        grid=(B, HW // width, n_t),
        in_specs=[spec] * len(args),
        out_specs=spec,
        out_shape=jax.ShapeDtypeStruct((B, T, HW), F32),
        scratch_shapes=[pltpu.VMEM((lane_groups, V7X_MXU_DIM, V7X_MXU_DIM), F32)],
        compiler_params=_params(("parallel", "parallel", "arbitrary")),
        name="delta_scan_rev" if reverse else "delta_scan_fwd",
    )(*args)


def _matmul_kernel(*refs, nk, passes, has_bias, a_silu):
    a_ref, b_ref = refs[0], refs[1]
    bias_ref = refs[2] if has_bias else None
    o_ref = refs[2 + has_bias]
    acc_ref = refs[3 + has_bias] if nk > 1 else None
    a = a_ref[...]
    if a_silu:
        a = a.astype(F32)
        a = a * jax.nn.sigmoid(a)
    if passes == 3:
        part = _dot3(a.astype(F32), b_ref[...].astype(F32))
    else:
        part = _bdot(a, b_ref[...])

    def finish(acc):
        if has_bias:
            acc = acc + bias_ref[...]
        o_ref[...] = acc.astype(o_ref.dtype)

    if nk == 1:
        finish(part)
    else:
        k = pl.program_id(2)

        @pl.when(k == 0)
        def _():
            acc_ref[...] = part

        @pl.when(k > 0)
        def _():
            acc_ref[...] += part

        @pl.when(k == nk - 1)
        def _():
            finish(acc_ref[...])


def matmul(a, b, *, tm, tn, tk=None, bias=None, out_dtype=F32, passes=1, a_silu=False, layer=None,
           name="matmul"):
    M, K = a.shape
    N = b.shape[-1]
    tk = K if tk is None else tk
    assert M % tm == 0 and N % tn == 0 and K % tk == 0, (a.shape, b.shape, tm, tn, tk)
    nk = K // tk
    if layer is None:
        b_spec = pl.BlockSpec((tk, tn), lambda i, j, k: (k, j))
    else:
        b_spec = pl.BlockSpec((pl.Squeezed(), tk, tn), lambda i, j, k: (layer, k, j))
    in_specs = [pl.BlockSpec((tm, tk), lambda i, j, k: (i, k)), b_spec]
    args = [a, b]
    if bias is not None:
        in_specs.append(pl.BlockSpec((1, tn), lambda i, j, k: (0, j)))
        args.append(bias.reshape(1, N).astype(F32))
    kern = functools.partial(_matmul_kernel, nk=nk, passes=passes, has_bias=bias is not None, a_silu=a_silu)
    return pl.pallas_call(
        kern,
        grid=(M // tm, N // tn, nk),
        in_specs=in_specs,
        out_specs=pl.BlockSpec((tm, tn), lambda i, j, k: (i, j)),
        out_shape=jax.ShapeDtypeStruct((M, N), out_dtype),
        scratch_shapes=[pltpu.VMEM((tm, tn), F32)] if nk > 1 else [],
        compiler_params=_params(("parallel", "parallel", "arbitrary")),
        name=name,
    )(*args)


def _pack_halves(h):
    half = h.shape[1] // 2
    bits = lax.bitcast_convert_type(h.astype(BF16).astype(F32), jnp.uint32)
    return (bits[:, half:] & jnp.uint32(0xFFFF0000)) | (bits[:, :half] >> 16)


def _unpack_halves(w):
    lo = lax.bitcast_convert_type(w << 16, F32).astype(BF16)
    hi = lax.bitcast_convert_type(w & jnp.uint32(0xFFFF0000), F32).astype(BF16)
    return lo, hi


def _norm_mod_kernel(*refs, which, kinds, res_gate):
    if res_gate is None:
        x_ref, g_ref, mod_ref, *o_refs = refs
        x = x_ref[0]
    else:
        x_ref, res_ref, g_ref, mod_ref, xo_ref, *o_refs = refs
        x = x_ref[0] + mod_ref[0, res_gate:res_gate + 1, :] * res_ref[0]
        xo_ref[0] = x
    y = x * lax.rsqrt(jnp.mean(x * x, axis=-1, keepdims=True) + NORM_EPS) * g_ref[...]
    shift = mod_ref[0, 3 * which:3 * which + 1, :]
    scale = mod_ref[0, 3 * which + 1:3 * which + 2, :]
    h = y * (1.0 + scale) + shift
    for kind, o_ref in zip(kinds, o_refs):
        o_ref[0] = _pack_halves(h) if kind == "packed" else h.astype(o_ref.dtype)


def norm_mod(x, g, mod, *, which, ctx_tiles, kinds=("bf16",), residual=None, res_gate=None):
    B, T, D = x.shape
    ctx_row = mod.shape[0] - 1
    sel = lambda b, j: jnp.where(j < ctx_tiles, ctx_row, b)
    tile = lambda w: pl.BlockSpec((1, ROW_TILE, w), lambda b, j: (b, j, 0))
    shapes = {"bf16": (D, BF16), "f32": (D, F32), "packed": (D // 2, jnp.uint32)}
    kinds_out = kinds if residual is None else ("f32",) + tuple(kinds)
    outs = pl.pallas_call(
        functools.partial(_norm_mod_kernel, which=which, kinds=kinds, res_gate=res_gate),
        grid=(B, T // ROW_TILE),
        in_specs=[tile(D)] * (1 if residual is None else 2) +
                 [pl.BlockSpec((1, D), lambda b, j: (0, 0)),
                  pl.BlockSpec((1, 6, D), lambda b, j: (sel(b, j), 0, 0))],
        out_specs=[tile(shapes[k][0]) for k in kinds_out],
        out_shape=[jax.ShapeDtypeStruct((B, T, shapes[k][0]), shapes[k][1]) for k in kinds_out],
        compiler_params=_params(("parallel", "parallel")),
        name="norm_mod",
    )(*([x] if residual is None else [x, residual]), g.reshape(1, D), mod)
    return outs[0] if len(kinds_out) == 1 else outs


def _route_kernel(t_ref, wt_ref, bias_ref, idx_ref, gate_ref, cnt_ref, run_ref):
    E = wt_ref.shape[0]
    tm = t_ref.shape[0]
    per = E // N_GROUPS

    @pl.when(pl.program_id(0) == 0)
    def _():
        run_ref[...] = jnp.zeros_like(run_ref)

    th, tl = _split2(t_ref[...])
    wh, wl = _split2(wt_ref[...])
    nt = lambda a, b: lax.dot_general(a, b, (((1,), (1,)), ((), ())), preferred_element_type=F32)
    logits = nt(wh, th) + (nt(wh, tl) + nt(wl, th))
    s = jax.nn.sigmoid(logits)
    sel = s + bias_ref[...]
    srow = [s[e:e + 1, :] for e in range(E)]
    row = [sel[e:e + 1, :] for e in range(E)]
    best_g, best_i = None, None
    for gq in range(N_GROUPS):
        a, b, c, d = row[gq * per:(gq + 1) * per]
        m1, n1, m2, n2 = jnp.maximum(a, b), jnp.minimum(a, b), jnp.maximum(c, d), jnp.minimum(c, d)
        score = jnp.maximum(m1, m2) + jnp.maximum(jnp.minimum(m1, m2), jnp.maximum(n1, n2))
        if gq == 0:
            best_g, best_i = score, jnp.zeros_like(score, dtype=jnp.int32)
        else:
            better = score > best_g
            best_g = jnp.where(better, score, best_g)
            best_i = jnp.where(better, gq, best_i)
    masked = [jnp.where(best_i == e // per, row[e], NEG_INF) for e in range(E)]

    def argmax_first(vals):
        top, idx = vals[0], jnp.zeros_like(best_i)
        for e in range(1, E):
            better = vals[e] > top
            top = jnp.where(better, vals[e], top)
            idx = jnp.where(better, e, idx)
        return idx

    i1 = argmax_first(masked)
    i2 = argmax_first([jnp.where(i1 == e, -jnp.inf, masked[e]) for e in range(E)])
    w1 = sum(jnp.where(i1 == e, srow[e], 0.0) for e in range(E))
    w2 = sum(jnp.where(i2 == e, srow[e], 0.0) for e in range(E))
    tot = w1 + w2
    chosen = jnp.concatenate([jnp.where((i1 == e) | (i2 == e), 1.0, 0.0) for e in range(E)], axis=0)
    earlier = lax.broadcasted_iota(jnp.int32, (tm, tm), 0) < lax.broadcasted_iota(jnp.int32, (tm, tm), 1)
    pos = run_ref[:, 0:1] + _bdot(chosen, earlier.astype(BF16))
    run_ref[...] = run_ref[...] + jnp.sum(chosen, axis=1, keepdims=True)
    p1 = sum(jnp.where(i1 == e, pos[e:e + 1, :], 0.0) for e in range(E)).astype(jnp.int32)
    p2 = sum(jnp.where(i2 == e, pos[e:e + 1, :], 0.0) for e in range(E)).astype(jnp.int32)
    zi = jnp.zeros((V7X_SUBLANES - 4, tm), jnp.int32)
    idx_ref[...] = jnp.concatenate([i1, i2, p1, p2, zi], axis=0)
    zf = jnp.zeros((V7X_SUBLANES - 2, tm), F32)
    gate_ref[...] = jnp.concatenate([w1 / tot, w2 / tot, zf], axis=0)
    cnt_ref[...] = run_ref[...]


def moe_route(t, router_w, router_bias, *, tm=512):
    n, D = t.shape
    E = router_w.shape[1]
    tm = min(tm, n)
    row8 = pl.BlockSpec((V7X_SUBLANES, tm), lambda i: (0, i))
    return pl.pallas_call(
        _route_kernel,
        grid=(n // tm,),
        in_specs=[pl.BlockSpec((tm, D), lambda i: (i, 0)),
                  pl.BlockSpec((E, D), lambda i: (0, 0)),
                  pl.BlockSpec((E, 1), lambda i: (0, 0))],
        out_specs=[row8, row8, pl.BlockSpec((E, V7X_LANES), lambda i: (0, 0))],
        out_shape=[jax.ShapeDtypeStruct((V7X_SUBLANES, n), jnp.int32),
                   jax.ShapeDtypeStruct((V7X_SUBLANES, n), F32),
                   jax.ShapeDtypeStruct((E, V7X_LANES), F32)],
        scratch_shapes=[pltpu.VMEM((E, V7X_LANES), F32)],
        compiler_params=_params(("arbitrary",)),
        name="moe_route",
    )(t, router_w.T, router_bias.reshape(E, 1))


MOE_TILE = 1152
COPY_ROWS = 256


def _dispatch_kernel(slot_ref, x_ref, xg_ref, sem):
    n = slot_ref.shape[0] // 2
    base = pl.program_id(0) * COPY_ROWS

    def copy(r, k):
        return pltpu.make_async_copy(x_ref.at[pl.ds(r, 1)], xg_ref.at[pl.ds(slot_ref[k * n + base + r], 1)], sem)

    def start(r, carry):
        copy(r, 0).start()
        copy(r, 1).start()
        return carry

    def wait(r, carry):
        copy(r, 0).wait()
        copy(r, 1).wait()
        return carry

    lax.fori_loop(0, COPY_ROWS, start, 0)
    lax.fori_loop(0, COPY_ROWS, wait, 0)


def moe_dispatch(slots, x_words, n_rows):
    n, W = x_words.shape
    return pl.pallas_call(
        _dispatch_kernel,
        grid_spec=pltpu.PrefetchScalarGridSpec(
            num_scalar_prefetch=1,
            grid=(n // COPY_ROWS,),
            in_specs=[pl.BlockSpec((COPY_ROWS, W), lambda i, s: (i, 0))],
            out_specs=pl.BlockSpec(memory_space=pl.ANY),
            scratch_shapes=[pltpu.SemaphoreType.DMA(())],
        ),
        out_shape=jax.ShapeDtypeStruct((n_rows, W), x_words.dtype),
        compiler_params=_params(("arbitrary",)),
        name="moe_dispatch",
    )(slots, x_words)


def _moe_up_kernel(te_ref, rows_ref, blk_ref, x_ref, wg_ref, wu_ref, o_ref, xs_ref):
    rows = rows_ref[pl.program_id(0)]
    half = x_ref.shape[1]

    @pl.when((rows > 0) & (pl.program_id(1) == 0))
    def _():
        valid = lax.broadcasted_iota(jnp.int32, x_ref.shape, 0) < rows
        lo, hi = _unpack_halves(jnp.where(valid, x_ref[...], jnp.uint32(0)))
        xs_ref[:, :half] = lo
        xs_ref[:, half:] = hi

    @pl.when(rows > 0)
    def _():
        x = xs_ref[...]
        hg = jnp.dot(x, wg_ref[...].astype(BF16), preferred_element_type=F32)
        hu = jnp.dot(x, wu_ref[...].astype(BF16), preferred_element_type=F32)
        o_ref[...] = (hg * jax.nn.sigmoid(hg) * hu).astype(o_ref.dtype)


def _moe_down_kernel(te_ref, rows_ref, blk_ref, a_ref, wd_ref, o_ref):
    @pl.when(rows_ref[pl.program_id(0)] > 0)
    def _():
        o_ref[...] = _bdot(a_ref[...], wd_ref[...])


def moe_experts(xg, w_gate, w_up, w_down, tile_expert, tile_rows, tile_block, *, layer, tm, tf=256, tn=1024):
    _, E, D, F = w_gate.shape
    nt = tile_expert.shape[0]
    sq = pl.Squeezed()
    nf, nc = F // tf, D // tn
    hold = lambda i, last, tr, j: jnp.where(tr[j] > 0, i, last)
    act = pl.pallas_call(
        _moe_up_kernel,
        grid_spec=pltpu.PrefetchScalarGridSpec(
            num_scalar_prefetch=3,
            grid=(nt, nf),
            in_specs=[pl.BlockSpec((tm, D // 2), lambda j, f, te, tr, tb: (tb[j], 0)),
                      pl.BlockSpec((sq, sq, D, tf), lambda j, f, te, tr, tb: (layer, te[j], 0, hold(f, nf - 1, tr, j))),
                      pl.BlockSpec((sq, sq, D, tf), lambda j, f, te, tr, tb: (layer, te[j], 0, hold(f, nf - 1, tr, j)))],
            out_specs=pl.BlockSpec((tm, tf), lambda j, f, te, tr, tb: (tb[j], hold(f, nf - 1, tr, j))),
            scratch_shapes=[pltpu.VMEM((tm, D), BF16)],
        ),
        out_shape=jax.ShapeDtypeStruct((nt * tm, F), BF16),
        compiler_params=_params(("arbitrary", "arbitrary")),
        name="moe_up",
    )(tile_expert, tile_rows, tile_block, xg, w_gate, w_up)
    return pl.pallas_call(
        _moe_down_kernel,
        grid_spec=pltpu.PrefetchScalarGridSpec(
            num_scalar_prefetch=3,
            grid=(nt, nc),
            in_specs=[pl.BlockSpec((tm, F), lambda j, c, te, tr, tb: (tb[j], 0)),
                      pl.BlockSpec((sq, sq, F, tn), lambda j, c, te, tr, tb: (layer, te[j], 0, hold(c, nc - 1, tr, j)))],
            out_specs=pl.BlockSpec((tm, tn), lambda j, c, te, tr, tb: (tb[j], hold(c, nc - 1, tr, j))),
        ),
        out_shape=jax.ShapeDtypeStruct((nt * tm, D), F32),
        compiler_params=_params(("arbitrary", "arbitrary")),
        name="moe_down",
    )(tile_expert, tile_rows, tile_block, act, w_down)


def _combine_kernel(slot_ref, y_ref, gate_ref, x_ref, mod_ref, *rest, res_gate, has_final):
    fin_ref = rest[0] if has_final else None
    o_ref, ya_ref, yb_ref, sem = rest[has_final:]
    n = slot_ref.shape[0] // 2
    base = pl.program_id(0) * COPY_ROWS

    def copy(r, k):
        dst = ya_ref if k == 0 else yb_ref
        return pltpu.make_async_copy(y_ref.at[pl.ds(slot_ref[k * n + base + r], 1)], dst.at[pl.ds(r, 1)], sem)

    def start(r, carry):
        copy(r, 0).start()
        copy(r, 1).start()
        return carry

    def wait(r, carry):
        copy(r, 0).wait()
        copy(r, 1).wait()
        return carry

    lax.fori_loop(0, COPY_ROWS, start, 0)
    lax.fori_loop(0, COPY_ROWS, wait, 0)
    g = gate_ref[...].T
    out = x_ref[...] + mod_ref[0, res_gate:res_gate + 1, :] * (ya_ref[...] * g[:, 0:1] + yb_ref[...] * g[:, 1:2])
    if fin_ref is not None:
        out = out * lax.rsqrt(jnp.mean(out * out, axis=-1, keepdims=True) + NORM_EPS) * fin_ref[...]
    o_ref[...] = out


def moe_combine(slots, y, gates, x, mod, *, res_gate, tiles_per_batch, ctx_tiles, final_g=None):
    n, D = x.shape
    ctx_row = mod.shape[0] - 1
    sel = lambda i: jnp.where(i % tiles_per_batch < ctx_tiles, ctx_row, i // tiles_per_batch)
    in_specs = [pl.BlockSpec(memory_space=pl.ANY),
                pl.BlockSpec((V7X_SUBLANES, COPY_ROWS), lambda i, s: (0, i)),
                pl.BlockSpec((COPY_ROWS, D), lambda i, s: (i, 0)),
                pl.BlockSpec((1, 6, D), lambda i, s: (sel(i), 0, 0))]
    args = [slots, y, gates, x, mod]
    if final_g is not None:
        in_specs.append(pl.BlockSpec((1, D), lambda i, s: (0, 0)))
        args.append(final_g.reshape(1, D))
    return pl.pallas_call(
        functools.partial(_combine_kernel, res_gate=res_gate, has_final=final_g is not None),
        grid_spec=pltpu.PrefetchScalarGridSpec(
            num_scalar_prefetch=1,
            grid=(n // COPY_ROWS,),
            in_specs=in_specs,
            out_specs=pl.BlockSpec((COPY_ROWS, D), lambda i, s: (i, 0)),
            scratch_shapes=[pltpu.VMEM((COPY_ROWS, D), F32), pltpu.VMEM((COPY_ROWS, D), F32),
                            pltpu.SemaphoreType.DMA(())],
        ),
        out_shape=jax.ShapeDtypeStruct((n, D), F32),
        compiler_params=_params(("arbitrary",)),
        name="moe_combine",
    )(*args)


def _na_kernel(rpb_ref, q_ref, k_ref, v_ref, kc_ref, vc_ref, o_ref, tb_ref):
    h = pl.program_id(1)
    GW, HD = GRID_W, V7X_LANES
    n_rows = q_ref.shape[1] // GW
    wr = NA_WIN_R
    n_dc = 2 * NA_WIN_C - 1
    qcol = lax.broadcasted_iota(jnp.int32, (GW, 2 * GW), 0)
    lane = lax.broadcasted_iota(jnp.int32, (GW, 2 * GW), 1)
    kcol = lane & (GW - 1)
    upper = lane >= GW
    dci = jnp.clip(kcol - qcol + NA_WIN_C - 1, 0, n_dc - 1)
    cstart = jnp.clip(qcol - NA_WIN_C // 2, 0, GW - NA_WIN_C)
    col_ok = (kcol >= cstart) & (kcol < cstart + NA_WIN_C)
    for dr in range(2 * wr - 2):
        acc = jnp.zeros((GW, 2 * GW), F32)
        for j in range(n_dc):
            lo = rpb_ref[h, dr * n_dc + j]
            hi = rpb_ref[h, (dr + 1) * n_dc + j]
            acc = jnp.where(dci == j, jnp.where(upper, hi, lo), acc)
        tb_ref[dr] = acc
    ok4 = jnp.concatenate([col_ok] * (wr // 2), axis=1)
    kc = kc_ref[0]
    vc = vc_ref[0]
    scale = HD ** -0.5

    def body(i, carry):
        rr = [i * NA_LOCK + u for u in range(NA_LOCK)]
        rs = [jnp.clip(r - wr // 2, 0, n_rows - wr) for r in rr]
        qrow = [pl.ds(pl.multiple_of(r * GW, GW), GW) for r in rr]
        krow = [pl.ds(pl.multiple_of(s * GW, GW), wr * GW) for s in rs]
        q = [q_ref[0, qr, :] * scale for qr in qrow]
        s_lat = [_bdot_nt(a, k_ref[0, kr, :]) for a, kr in zip(q, krow)]
        s_ctx = [_bdot_nt(a, kc) for a in q]
        bias = [jnp.concatenate([tb_ref[s - r + wr - 1 + 2 * w] for w in range(wr // 2)], axis=1)
                for r, s in zip(rr, rs)]
        s_lat = [jnp.where(ok4, a + b, NEG_INF) for a, b in zip(s_lat, bias)]
        m = [jnp.maximum(jnp.max(a, axis=-1, keepdims=True), jnp.max(b, axis=-1, keepdims=True))
             for a, b in zip(s_lat, s_ctx)]
        p_lat = [jnp.exp(a - c) for a, c in zip(s_lat, m)]
        p_ctx = [jnp.exp(a - c) for a, c in zip(s_ctx, m)]
        den = [jnp.sum(a, axis=-1, keepdims=True) + jnp.sum(b, axis=-1, keepdims=True)
               for a, b in zip(p_lat, p_ctx)]
        o_lat = [_bdot(a, v_ref[0, kr, :]) for a, kr in zip(p_lat, krow)]
        o_ctx = [_bdot(a, vc) for a in p_ctx]
        for qr, a, b, d in zip(qrow, o_lat, o_ctx, den):
            o_ref[0, qr, :] = (a + b) / d
        return carry

    lax.fori_loop(0, n_rows // NA_LOCK, body, 0)


def neighbourhood_attention(pl_lat, kv_ctx, rpb):
    B, S, _ = pl_lat.shape
    Lc = kv_ctx.shape[1]
    H, HD = NA_HEADS, V7X_LANES
    lat = lambda off: pl.BlockSpec((1, S, HD), lambda b, h: (b, 0, off + h))
    cx = lambda off: pl.BlockSpec((1, Lc, HD), lambda b, h: (b, 0, off + h))
    return pl.pallas_call(
        _na_kernel,
        grid=(B, H),
        in_specs=[pl.BlockSpec(memory_space=pltpu.SMEM), lat(0), lat(H), lat(2 * H), cx(0), cx(H)],
        out_specs=pl.BlockSpec((1, S, HD), lambda b, h: (b, 0, h)),
        out_shape=jax.ShapeDtypeStruct((B, S, H * HD), F32),
        scratch_shapes=[pltpu.VMEM((2 * NA_WIN_R - 2, GRID_W, 2 * GRID_W), F32)],
        compiler_params=_params(("parallel", "parallel")),
        name="neighbourhood_attention",
    )(rpb.reshape(H, -1), pl_lat, pl_lat, pl_lat, kv_ctx, kv_ctx)


DFT_N1 = 64
DFT_N2 = 128
DFT_SUB = 8
DFT_LOCK = 4


@functools.lru_cache(maxsize=None)
def _dft_constants():
    n1, n2 = DFT_N1, DFT_N2
    n = n1 * n2
    a = np.arange(n1)
    f1 = np.exp(-2j * np.pi * np.outer(a, a) / n1)
    half = f1[:, :n1 // 2]
    eye = np.eye(DFT_SUB)
    fwd = np.kron(half, eye)
    inv = np.kron(np.conj(f1)[:n1 // 2, :], eye) / n
    b = np.arange(n2)
    f2 = np.exp(-2j * np.pi * np.outer(b, b) / n2)
    tw = np.exp(-2j * np.pi * np.outer(a, b) / n)
    m = f2[None, :, :] * tw[:, None, :]
    mt = np.transpose(m, (0, 2, 1))
    c = lambda z: (np.ascontiguousarray(z.real, dtype=np.float32), np.ascontiguousarray(z.imag, dtype=np.float32))
    return c(fwd) + c(inv) + c(m) + c(mt)


def _hyena_conv_kernel(hf_ref, hb_ref, z_ref, fwr_ref, fwi_ref, ivr_ref, ivi_ref,
                       mr_ref, mi_ref, mtr_ref, mti_ref, o_ref, ar_ref, ai_ref, kr_ref, ki_ref):
    L, TC = hf_ref.shape
    N1, N2, SUB = DFT_N1, DFT_N2, DFT_SUB
    nblk = N2 // SUB

    def stage1(load):
        def blk(nb, carry):
            off = pl.multiple_of(nb * SUB, SUB)
            xs = jnp.concatenate([load(n1 * N2 + off) for n1 in range(N1 // 2)], axis=0)
            yr = _bdot(fwr_ref[...], xs)
            yi = _bdot(fwi_ref[...], xs)
            for k1 in range(N1):
                ar_ref[pl.ds(k1 * N2 + off, SUB), :] = yr[k1 * SUB:(k1 + 1) * SUB]
                ai_ref[pl.ds(k1 * N2 + off, SUB), :] = yi[k1 * SUB:(k1 + 1) * SUB]
            return carry

        lax.fori_loop(0, nblk, blk, 0)

    def cmul_many(m_r, m_i, x_r, x_i, conj):
        rr = [_bdot(a, b) for a, b in zip(m_r, x_r)]
        ii = [_bdot(a, b) for a, b in zip(m_i, x_i)]
        ri = [_bdot(a, b) for a, b in zip(m_r, x_i)]
        ir = [_bdot(a, b) for a, b in zip(m_i, x_r)]
        if conj:
            return [a + b for a, b in zip(rr, ii)], [a - b for a, b in zip(ri, ir)]
        return [a - b for a, b in zip(rr, ii)], [a + b for a, b in zip(ri, ir)]

    def spectrum(i):
        k1s = [i * DFT_LOCK + u for u in range(DFT_LOCK)]
        rows = [pl.ds(pl.multiple_of(k1 * N2, N2), N2) for k1 in k1s]
        x_r, x_i = cmul_many([mr_ref[k1] for k1 in k1s], [mi_ref[k1] for k1 in k1s],
                             [ar_ref[r, :] for r in rows], [ai_ref[r, :] for r in rows], conj=False)
        return k1s, rows, x_r, x_i

    stage1(lambda r0: hf_ref[pl.ds(r0, SUB), :])

    def filt_f(i, carry):
        _, rows, x_r, x_i = spectrum(i)
        for r, a, b in zip(rows, x_r, x_i):
            kr_ref[r, :] = a
            ki_ref[r, :] = b
        return carry

    lax.fori_loop(0, N1 // DFT_LOCK, filt_f, 0)
    sub_iota = lax.broadcasted_iota(jnp.int32, (SUB, TC), 0)
    stage1(lambda r0: jnp.where(sub_iota + r0 == 0, 0.0, hb_ref[pl.ds(r0, SUB), :]))

    def filt_b(i, carry):
        _, rows, x_r, x_i = spectrum(i)
        for r, a, b in zip(rows, x_r, x_i):
            kr_ref[r, :] += a
            ki_ref[r, :] -= b
        return carry

    lax.fori_loop(0, N1 // DFT_LOCK, filt_b, 0)

    for b in range(z_ref.shape[0]):
        stage1(lambda r0: z_ref[b, pl.ds(r0, SUB), :])

        def conv(i, carry):
            k1s, rows, x_r, x_i = spectrum(i)
            f_r, f_i = [kr_ref[r, :] for r in rows], [ki_ref[r, :] for r in rows]
            y_r = [a * c - b_ * d for a, b_, c, d in zip(x_r, x_i, f_r, f_i)]
            y_i = [a * d + b_ * c for a, b_, c, d in zip(x_r, x_i, f_r, f_i)]
            o_r, o_i = cmul_many([mtr_ref[k1] for k1 in k1s], [mti_ref[k1] for k1 in k1s], y_r, y_i, conj=True)
            for r, a, c in zip(rows, o_r, o_i):
                ar_ref[r, :] = a
                ai_ref[r, :] = c
            return carry

        lax.fori_loop(0, N1 // DFT_LOCK, conv, 0)
        def last(nb, carry):
            off = pl.multiple_of(nb * SUB, SUB)
            br = jnp.concatenate([ar_ref[pl.ds(k1 * N2 + off, SUB), :] for k1 in range(N1)], axis=0)
            bi = jnp.concatenate([ai_ref[pl.ds(k1 * N2 + off, SUB), :] for k1 in range(N1)], axis=0)
            y = _bdot(ivr_ref[...], br) - _bdot(ivi_ref[...], bi)
            for n1 in range(N1 // 2):
                o_ref[b, pl.ds(n1 * N2 + off, SUB), :] = y[n1 * SUB:(n1 + 1) * SUB]
            return carry

        lax.fori_loop(0, nblk, last, 0)


def hyena_long_conv(z, h_f, h_b, *, tc=V7X_LANES):
    B, L, C = z.shape
    assert L == DFT_N1 * DFT_N2 // 2
    consts = [jnp.asarray(a, BF16) for a in _dft_constants()]
    whole = lambda a: pl.BlockSpec(a.shape, lambda j: (0,) * a.ndim)
    col = pl.BlockSpec((L, tc), lambda j: (0, j))
    bcol = pl.BlockSpec((B, L, tc), lambda j: (0, 0, j))
    return pl.pallas_call(
        _hyena_conv_kernel,
        grid=(C // tc,),
        in_specs=[col, col, bcol] + [whole(a) for a in consts],
        out_specs=bcol,
        out_shape=jax.ShapeDtypeStruct((B, L, C), F32),
        scratch_shapes=[pltpu.VMEM((2 * L, tc), F32)] * 4,
        compiler_params=_params(("parallel",)),
        name="hyena_long_conv",
    )(h_f, h_b, z, *consts)


def _pick(n, options):
    for t in options:
        if n % t == 0:
            return t
    raise ValueError(f"no tile for {n} in {options}")


def _mm(a, w, **kw):
    tm = _pick(a.shape[0], (1088, 1024, 768, 512, 256, 8))
    tn = _pick(w.shape[1], (512, 640, 384, 256, 128))
    return matmul(a, w, tm=tm, tn=tn, **kw)


def _pad_to(a, axis, mult):
    pad = (-a.shape[axis]) % mult
    if pad == 0:
        return a
    widths = [(0, 0)] * a.ndim
    widths[axis] = (0, pad)
    return jnp.pad(a, widths)


def _silu(x):
    return x * jax.nn.sigmoid(x)


def _seq_parts(x, lc, fn):
    if lc == 0:
        return fn(x)
    return jnp.concatenate([fn(x[:, :lc]), fn(x[:, lc:])], axis=1)


def _dwconv(x, w):
    k, length = w.shape[0], x.shape[1]
    xp = jnp.pad(x, ((0, 0), (k // 2, k // 2), (0, 0)))
    return sum(xp[:, j:j + length] * w[j] for j in range(k))


def _centred_shift(p):
    prev = jnp.pad(p, ((0, 0), (1, 0), (0, 0)))[:, :-1]
    nxt = jnp.pad(p, ((0, 0), (0, 1), (0, 0)))[:, 1:]
    return 0.5 * (prev + nxt)


def _l2norm(t):
    return t * lax.rsqrt(jnp.sum(t * t, -1, keepdims=True) + 1e-6)


def _axial_rope(t):
    length, d = t.shape[1], t.shape[-1]
    nf = d // 4
    pos = jnp.arange(length)
    inv = ROPE_BASE ** (-jnp.arange(nf, dtype=F32) / nf)

    def cs(p):
        ang = p.astype(F32)[:, None] * inv[None, :]
        return jnp.cos(ang), jnp.sin(ang)

    cr, sr = cs(pos // GRID_W)
    cc, sc = cs(pos % GRID_W)
    cos = jnp.concatenate([cr, cr, cc, cc], -1)[None, :, None, :]
    sin = jnp.concatenate([-sr, sr, -sc, sc], -1)[None, :, None, :]
    swapped = jnp.flip(t.reshape(*t.shape[:-1], 2, 2, nf), axis=-2).reshape(t.shape)
    return t * cos + swapped * sin


def _mixer_ab(h, lc, w_in, dn_conv_w, dn_a_log, dn_dt_bias, dn_norm_w,
              rw_mu, rw_w0, rw_w2, rw_a0, rw_a2, rw_g2, rw_k_k, rw_k_a, rw_r_k, rw_ln_w, rw_ln_b, w_out):
    B, T, D = h.shape
    mw = D // 2
    hd_dn = mw // DN_HEADS
    h_rw = mw // RW_HD
    p_a = 4 * mw + 4 * DN_HEADS
    h2 = h.reshape(B * T, D)
    pa = _mm(h2, _pad_to(w_in[:, :p_a], 1, V7X_LANES).astype(BF16)).reshape(B, T, -1)
    pb = _mm(h2, _pad_to(w_in[:, p_a:], 1, V7X_LANES).astype(BF16)).reshape(B, T, -1)

    qkv = _seq_parts(pa[..., :3 * mw], lc, lambda u: _silu(_dwconv(u, dn_conv_w)))
    q, k, v = (t.reshape(B, T, DN_HEADS, hd_dn) for t in jnp.split(qkv, 3, axis=-1))
    rope_lat = lambda t: jnp.concatenate([t[:, :lc], _axial_rope(t[:, lc:])], axis=1)
    q, k = rope_lat(_l2norm(q)), rope_lat(_l2norm(k))
    q = q * hd_dn ** -0.5
    z = pa[..., 3 * mw:4 * mw]
    bg = pa[..., 4 * mw:p_a].reshape(B, T, 4, DN_HEADS)
    beta = jax.nn.sigmoid(bg[:, :, :2])
    g = -jnp.exp(dn_a_log) * jax.nn.softplus(bg[:, :, 2:] + dn_dt_bias)
    flat = lambda t: t.reshape(B, T, mw)
    o = 0.0
    for d in range(2):
        bk = k * beta[:, :, d, :, None]
        gd = jnp.broadcast_to(g[:, :, d, :, None], k.shape)
        o = o + delta_scan(flat(q), flat(k), None, flat(bk), flat(v), flat(gd),
                           head_dim=hd_dn, reverse=d == 1, scalar_decay=True, beta_excl=False)
    o = o.reshape(B, T, DN_HEADS, hd_dn)
    o = o * lax.rsqrt(jnp.mean(o * o, -1, keepdims=True) + NORM_EPS) * dn_norm_w
    dn = flat(o * _silu(z).reshape(o.shape))

    p_b = w_in.shape[1] - p_a
    p = pb[..., :p_b]
    p = p + (_seq_parts(p, lc, _centred_shift) - p) * rw_mu
    r, k, v = p[..., :mw], p[..., mw:2 * mw], p[..., 2 * mw:3 * mw]
    o1 = 3 * mw
    n_w, n_a = rw_w2.shape[1], rw_a2.shape[1]
    o2 = o1 + 2 * n_w
    o3 = o2 + 2 * n_a
    wl = p[..., o1:o2].reshape(B * T, 2, n_w)
    al = p[..., o2:o3].reshape(B * T, 2, n_a)
    gl = p[..., o3:].reshape(B * T, -1)
    gate = _mm(_pad_to(jax.nn.sigmoid(gl), 1, V7X_LANES), _pad_to(rw_g2, 0, V7X_LANES)).reshape(B, T, mw)
    heads = lambda t: t.reshape(B, T, h_rw, RW_HD)
    kk = flat(_l2norm(heads(k * rw_k_k)))
    y = 0.0
    for d in range(2):
        w = -jax.nn.softplus(-(rw_w0[d] + _mm(jnp.tanh(wl[:, d]), rw_w2[d]).reshape(B, T, mw))) - 0.5
        a = jax.nn.sigmoid(rw_a0[d] + _mm(al[:, d], rw_a2[d]).reshape(B, T, mw))
        kd = k * (1.0 + (a - 1.0) * rw_k_a)
        y = y + delta_scan(r, kk, -(kk * a), kd, v, -jnp.exp(w),
                           head_dim=RW_HD, reverse=d == 1, scalar_decay=False, beta_excl=True)
    y = heads(y)
    mu = jnp.mean(y, -1, keepdims=True)
    var = jnp.mean(jnp.square(y - mu), -1, keepdims=True)
    yn = flat((y - mu) * lax.rsqrt(var + RW_LN_EPS)) * rw_ln_w + rw_ln_b
    bonus = flat(jnp.sum(heads(r) * heads(k) * rw_r_k, -1, keepdims=True) * heads(v))
    rw = (yn + bonus) * gate

    cat = jnp.concatenate([dn, rw], -1).astype(BF16).reshape(B * T, D)
    return _mm(cat, w_out.astype(BF16)).reshape(B, T, D)


def _hyena_filters(length, ch, w1, b1, freq, w2, b2, w3, b3, w4):
    t = jnp.linspace(0.0, 1.0, length, dtype=F32)
    bands = (HY_EMB - 1) // 2
    wpos = 2 * math.pi * jnp.arange(length, dtype=F32) / length
    fb = jnp.linspace(1e-4, bands - 1, bands, dtype=F32)
    ang = wpos[:, None] * fb[None, :]
    z = jnp.concatenate([t[:, None], jnp.cos(ang), -jnp.sin(ang)], axis=-1)
    dense = lambda u, w: _mm(_pad_to(u, 1, V7X_LANES), _pad_to(_pad_to(w, 0, V7X_LANES), 1, V7X_LANES),
                             passes=3)[:, :w.shape[1]]
    h = jnp.sin(freq * (dense(z, w1) + b1))
    h = jnp.sin(freq * (dense(h, w2) + b2))
    h = jnp.sin(freq * (dense(h, w3) + b3))
    h = dense(h, w4).reshape(length, 2, ch)
    deltas = jnp.abs(jnp.linspace(HY_MIN_DECAY, HY_MAX_DECAY, ch, dtype=F32))
    window = jnp.exp(-t[:, None] * deltas[None, :])
    return h[:, 0] * window, h[:, 1] * window


def _mixer_cd(hc, hl, w_in, na_rpb, hy_conv_w, hy_conv_b, filt, hy_skip, w_out):
    B, S, D = hl.shape
    lc = hc.shape[1]
    mw = D // 2
    w_bf = w_in.astype(BF16)
    pl_ = _mm(hl.reshape(B * S, D), w_bf).reshape(B, S, -1)
    kv_c = _mm(hc.reshape(B * lc, D), w_bf[:, mw:3 * mw]).reshape(B, lc, 2 * mw)
    att = neighbourhood_attention(pl_, kv_c, na_rpb)
    u = _dwconv(pl_[..., 3 * mw:], hy_conv_w) + hy_conv_b
    x0, x1, v = jnp.split(u, 3, axis=-1)
    h_f, h_b = _hyena_filters(S, mw, *filt)
    z = x1 * v
    hy = x0 * (hyena_long_conv(z, h_f, h_b) + z * hy_skip)
    cat = jnp.concatenate([att, hy], -1).astype(BF16).reshape(B * S, D)
    return _mm(cat, w_out.astype(BF16)).reshape(B, S, D)


def _moe(t_packed, t_f32, router_w, router_bias, w_gate, w_up, w_down, layer, x_res, tail, tile=MOE_TILE):
    n, D = t_f32.shape
    E = w_gate.shape[1]
    idx, gates, counts = moe_route(t_f32, router_w, router_bias)
    cnt = counts[:, 0].astype(jnp.int32)
    nt = -(-2 * n // tile) + E
    tiles = (cnt + tile - 1) // tile
    ends = jnp.cumsum(tiles)
    starts = ends - tiles
    used = ends[-1]
    j = jnp.arange(nt, dtype=jnp.int32)
    jj = jnp.minimum(j, used - 1)
    tile_expert = jnp.sum((jj[:, None] >= ends[None, :]).astype(jnp.int32), axis=1)
    tile_rows = jnp.where(j < used, jnp.clip(cnt[tile_expert] - (jj - starts[tile_expert]) * tile, 0, tile), 0)
    row0 = starts * tile
    slots = jnp.concatenate([row0[idx[0]] + idx[2], row0[idx[1]] + idx[3]]).astype(jnp.int32)
    xg = moe_dispatch(slots, t_packed, nt * tile)
    y = moe_experts(xg, w_gate, w_up, w_down, tile_expert.astype(jnp.int32), tile_rows.astype(jnp.int32), jj,
                    layer=layer, tm=tile)
    return moe_combine(slots, y, gates, x_res, **tail)


def kernel(x, c, ctx, c_ctx, ada_w, ada_b, norm1_g, norm2_g, final_g, ab_w_in, dn_conv_w, dn_a_log, dn_dt_bias, dn_norm_w, rw_mu, rw_w0, rw_w2, rw_a0, rw_a2, rw_g2, rw_k_k, rw_k_a, rw_r_k, rw_ln_w, rw_ln_b, ab_w_out, cd_w_in, na_rpb, hy_conv_w, hy_conv_b, hy_w1, hy_b1, hy_freq, hy_w2, hy_b2, hy_w3, hy_b3, hy_w4, hy_skip, cd_w_out, router_w, router_bias, moe_w_gate, moe_w_up, moe_w_down):
    B, S, D = x.shape
    lc = ctx.shape[1]
    assert ada_w.shape[0] == 2 and lc == ROW_TILE and S % ROW_TILE == 0 and B + 1 <= V7X_SUBLANES
    T = lc + S
    ctx_tiles = lc // ROW_TILE
    cond = jnp.concatenate([c, c_ctx[None], jnp.zeros((V7X_SUBLANES - B - 1, D), F32)], axis=0)
    mods = [matmul(cond, ada_w, tm=V7X_SUBLANES, tn=512, bias=ada_b[l], a_silu=True, layer=l,
                   name="adaln")[:B + 1].reshape(B + 1, 6, D) for l in range(2)]

    m = mods[0]
    xa = jnp.concatenate([ctx, x], axis=1)
    h = norm_mod(xa, norm1_g[0], m, which=0, ctx_tiles=ctx_tiles)
    mix = _mixer_ab(h, lc, ab_w_in[0], dn_conv_w[0], dn_a_log[0], dn_dt_bias[0], dn_norm_w[0],
                    rw_mu[0], rw_w0[0], rw_w2[0], rw_a0[0], rw_a2[0], rw_g2[0], rw_k_k[0], rw_k_a[0],
                    rw_r_k[0], rw_ln_w[0], rw_ln_b[0], ab_w_out[0])
    xa, h_pk, h_f32 = norm_mod(xa, norm2_g[0], m, which=1, ctx_tiles=ctx_tiles, kinds=("packed", "f32"),
                               residual=mix, res_gate=2)
    tail = dict(mod=m, res_gate=5, tiles_per_batch=T // COPY_ROWS, ctx_tiles=ctx_tiles)
    xa = _moe(h_pk.reshape(B * T, D // 2), h_f32.reshape(B * T, D), router_w, router_bias,
              moe_w_gate, moe_w_up, moe_w_down, 0, xa.reshape(B * T, D), tail).reshape(B, T, D)

    m = mods[1]
    cx, xl = xa[:, :lc], xa[:, lc:]
    hl = norm_mod(xl, norm1_g[1], m, which=0, ctx_tiles=0)
    hc = norm_mod(cx, norm1_g[1], m, which=0, ctx_tiles=ctx_tiles)
    filt = (hy_w1[0], hy_b1[0], hy_freq[0], hy_w2[0], hy_b2[0], hy_w3[0], hy_b3[0], hy_w4[0])
    ml = _mixer_cd(hc, hl, cd_w_in[0], na_rpb[0], hy_conv_w[0], hy_conv_b[0], filt, hy_skip[0], cd_w_out[0])
    xl, h_pk, h_f32 = norm_mod(xl, norm2_g[1], m, which=1, ctx_tiles=0, kinds=("packed", "f32"),
                               residual=ml, res_gate=2)
    tail = dict(mod=m, res_gate=5, tiles_per_batch=S // COPY_ROWS, ctx_tiles=0, final_g=final_g)
    return _moe(h_pk.reshape(B * S, D // 2), h_f32.reshape(B * S, D), router_w, router_bias,
                moe_w_gate, moe_w_up, moe_w_down, 1, xl.reshape(B * S, D), tail).reshape(B, S, D)
```

```python
import functools
import math

import jax
import jax.numpy as jnp
import numpy as np
from jax import lax
from jax.experimental import pallas as pl
from jax.experimental.pallas import tpu as pltpu

F32 = jnp.float32
BF16 = jnp.bfloat16

V7X_LANES = 128
V7X_SUBLANES = 8
V7X_MXU_DIM = 256
V7X_VMEM_LIMIT_BYTES = 56 * 1024 * 1024

GRID_W = 64
NORM_EPS = 1e-6
NEG_INF = -1e30
ROPE_BASE = 10000.0
DN_HEADS = 16
DN_CONV = 5
RW_HD = 64
RW_LN_EPS = 64e-5
NA_HEADS = 16
NA_WIN_R = 8
NA_WIN_C = 16
NA_LOCK = 4
HY_EMB = 33
HY_TARGET = 1e-2
HY_MAX_DECAY = math.log(HY_TARGET) / 0.3
HY_MIN_DECAY = math.log(HY_TARGET) / 1.5
N_EXPERTS = 16
N_GROUPS = 4

ROW_TILE = 256
SCAN_CHUNK = 64


def _params(semantics, vmem=V7X_VMEM_LIMIT_BYTES):
    return pltpu.CompilerParams(dimension_semantics=semantics, vmem_limit_bytes=vmem)


def _bdot(a, b):
    return jnp.dot(a.astype(BF16), b.astype(BF16), preferred_element_type=F32)


def _bdot_nt(a, b):
    return lax.dot_general(a.astype(BF16), b.astype(BF16), (((1,), (1,)), ((), ())),
                           preferred_element_type=F32)


def _split2(x):
    hi = x.astype(BF16)
    lo = (x - hi.astype(F32)).astype(BF16)
    return hi, lo


def _split3(x):
    hi = x.astype(BF16)
    r1 = x - hi.astype(F32)
    mid = r1.astype(BF16)
    lo = (r1 - mid.astype(F32)).astype(BF16)
    return hi, mid, lo


def _dot3(a, b):
    ah, al = _split2(a)
    bh, bl = _split2(b)
    d = lambda u, v: jnp.dot(u, v, preferred_element_type=F32)
    return d(ah, bh) + (d(ah, bl) + d(al, bh))


def _dot_exact_left(m, x):
    hi, mid, lo = _split3(x)
    d = lambda v: jnp.dot(m, v, preferred_element_type=F32)
    return d(hi) + (d(mid) + d(lo))


def _unit_tri_inverse(mats, row, col):
    eye = (row == col).astype(F32)
    in8 = (row >> 3) == (col >> 3)
    a8 = [jnp.where(in8, a, 0.0) for a in mats]
    a2 = [_bdot(u, u) for u in a8]
    p = [eye + u for u in a8]
    p = [u + _bdot(u, w) for u, w in zip(p, a2)]
    a4 = [_bdot(u, u) for u in a2]
    x = [u + _bdot(u, w) for u, w in zip(p, a4)]
    for sh in (3, 4, 5):
        m = ((row >> (sh + 1)) == (col >> (sh + 1))) & ((row >> sh) != (col >> sh))
        left = [_bdot(u, jnp.where(m, a, 0.0)) for u, a in zip(x, mats)]
        x = [u + _bdot(w, u) for u, w in zip(x, left)]
    return x


def _delta_scan_kernel(*refs, lane_groups, head_dim, reverse, scalar_decay, beta_excl, ka_is_neg_kk):
    if ka_is_neg_kk:
        r_ref, kb_ref, kk_ref, v_ref, g_ref, y_ref, t_ref = refs
        ka_ref = None
    else:
        r_ref, kb_ref, ka_ref, kk_ref, v_ref, g_ref, y_ref, t_ref = refs
    TT, W, C = ROW_TILE, V7X_MXU_DIM, SCAN_CHUNK
    n_heads = W // head_dim
    hshift = int(math.log2(head_dim))
    groups = range(lane_groups)
    units = [(gs, j) for gs in groups for j in range(n_heads)]

    @pl.when(pl.program_id(2) == 0)
    def _():
        t_ref[...] = jnp.zeros_like(t_ref)

    row = lax.broadcasted_iota(jnp.int32, (TT, TT), 0)
    col = lax.broadcasted_iota(jnp.int32, (TT, TT), 1)
    same = (row >> 6) == (col >> 6)
    if reverse:
        strict, incl = same & (col > row), same & (col >= row)
    else:
        strict, incl = same & (col < row), same & (col <= row)
    incl_bf, same_bf = incl.astype(BF16), same.astype(BF16)
    load = lambda ref: [ref[0, :, gs * W:(gs + 1) * W] for gs in groups]

    g = load(g_ref)
    gi = [_dot_exact_left(incl_bf, u) for u in g]
    gc = [_dot_exact_left(same_bf, u) for u in g]
    gb = [u - w for u, w in zip(gi, g)] if beta_excl else gi
    r, kb, kk, v = load(r_ref), load(kb_ref), load(kk_ref), load(v_ref)
    ka = [-u for u in kk] if ka_is_neg_kk else load(ka_ref)
    e_rest = [jnp.exp(c - i) for c, i in zip(gc, gi)]
    r_s = [u * jnp.exp(i) for u, i in zip(r, gi)]
    kb_s = [u * jnp.exp(i) for u, i in zip(kb, gb)]
    ka_e = [u * e for u, e in zip(ka, e_rest)]
    kk_e = [u * e for u, e in zip(kk, e_rest)]
    if scalar_decay:
        assert head_dim == V7X_LANES
        gram_r, gram_b, gram_k = r, kb, kk
        gram_a = None if ka_is_neg_kk else ka
    else:
        inv = [jnp.exp(-i) for i in gi]
        gram_r, gram_b = r_s, kb_s
        gram_k = [u * w for u, w in zip(kk, inv)]
        gram_a = None if ka_is_neg_kk else [u * w for u, w in zip(ka, inv)]

    lane_head = lax.broadcasted_iota(jnp.int32, (TT, W), 1) >> hshift
    mj = [lane_head == j for j in range(n_heads)]
    bl = [jnp.where(mj[j], gram_b[gs], 0.0) for gs, j in units]
    rl = [jnp.where(mj[j], gram_r[gs], 0.0) for gs, j in units]
    a_bk = [_bdot_nt(bl[u], gram_k[gs]) for u, (gs, j) in enumerate(units)]
    a_rk = [_bdot_nt(rl[u], gram_k[gs]) for u, (gs, j) in enumerate(units)]
    if ka_is_neg_kk:
        a_ba, a_ra = [-u for u in a_bk], [-u for u in a_rk]
    else:
        a_ba = [_bdot_nt(bl[u], gram_a[gs]) for u, (gs, j) in enumerate(units)]
        a_ra = [_bdot_nt(rl[u], gram_a[gs]) for u, (gs, j) in enumerate(units)]
    if scalar_decay:
        for u, (gs, j) in enumerate(units):
            sl = slice(j * head_dim, (j + 1) * head_dim)
            gi_m = jnp.concatenate([gi[gs][:, sl]] * n_heads, axis=1)
            gb_m = jnp.concatenate([gb[gs][:, sl]] * n_heads, axis=1)
            gi_t = gi_m.T
            d_b = jnp.exp(jnp.minimum(gb_m - gi_t, 0.0))
            d_r = jnp.exp(jnp.minimum(gi_m - gi_t, 0.0))
            a_ba[u], a_bk[u], a_ra[u], a_rk[u] = a_ba[u] * d_b, a_bk[u] * d_b, a_ra[u] * d_r, a_rk[u] * d_r
    a_ba = [jnp.where(strict, u, 0.0) for u in a_ba]
    a_bk = [jnp.where(strict, u, 0.0) for u in a_bk]
    a_ra = [jnp.where(incl, u, 0.0) for u in a_ra]
    a_rk = [jnp.where(incl, u, 0.0) for u in a_rk]
    x = _unit_tri_inverse(a_ba, row, col)
    bj = [_bdot(x[u], kb_s[gs]) for u, (gs, j) in enumerate(units)]
    t1 = [_bdot(a_bk[u], v[gs]) for u, (gs, j) in enumerate(units)]
    wj = [_bdot(x[u], t1[u]) for u in range(len(units))]
    ra = [_bdot(a_ra[u], bj[u]) for u in range(len(units))]
    ya = [_bdot(a_ra[u], wj[u]) for u in range(len(units))]
    yb = [_bdot(a_rk[u], v[gs]) for u, (gs, j) in enumerate(units)]
    b_hat, w1, r_hat, y0 = [], [], [], []
    for gs in groups:
        bh = wh = yh = jnp.zeros((TT, W), F32)
        rh = r_s[gs]
        for j in range(n_heads):
            u = gs * n_heads + j
            bh = jnp.where(mj[j], bj[u], bh)
            wh = jnp.where(mj[j], wj[u], wh)
            rh = rh + jnp.where(mj[j], ra[u], 0.0)
            yh = jnp.where(mj[j], ya[u] + yb[u], yh)
        b_hat.append(bh)
        w1.append(wh)
        r_hat.append(rh)
        y0.append(yh)

    rw = lax.broadcasted_iota(jnp.int32, (W, W), 0)
    cw = lax.broadcasted_iota(jnp.int32, (W, W), 1)
    blockdiag = (rw >> hshift) == (cw >> hshift)
    eye_w = rw == cw
    t = [t_ref[gs] for gs in groups]
    for c in (range(TT // C - 1, -1, -1) if reverse else range(TT // C)):
        sl = slice(c * C, (c + 1) * C)
        for gs in groups:
            y_ref[0, sl, gs * W:(gs + 1) * W] = _bdot(r_hat[gs][sl], t[gs]) + y0[gs][sl]
        ka_t = [ka_e[gs][sl].T for gs in groups]
        kk_t = [kk_e[gs][sl].T for gs in groups]
        decay_col = [jnp.sum(jnp.where(eye_w, jnp.exp(gc[gs][c * C:c * C + 1, :]), 0.0), axis=1, keepdims=True)
                     for gs in groups]
        mix = [jnp.where(blockdiag, _bdot(ka_t[gs], b_hat[gs][sl]), 0.0) for gs in groups]
        add = [jnp.where(blockdiag, _bdot(ka_t[gs], w1[gs][sl]) + _bdot(kk_t[gs], v[gs][sl]), 0.0) for gs in groups]
        t = [decay_col[gs] * t[gs] + _bdot(mix[gs], t[gs]) + add[gs] for gs in groups]
    for gs in groups:
        t_ref[gs] = t[gs]


def delta_scan(r, kb, ka, kk, v, g, *, head_dim, reverse, scalar_decay, beta_excl, lane_groups=2):
    B, T, HW = r.shape
    n_t = T // ROW_TILE
    width = lane_groups * V7X_MXU_DIM
    if reverse:
        tile = lambda i: jnp.where(i == 0, 0, n_t - i)
    else:
        tile = lambda i: i
    spec = pl.BlockSpec((1, ROW_TILE, width), lambda b, h, i: (b, tile(i), h))
    kern = functools.partial(_delta_scan_kernel, lane_groups=lane_groups, head_dim=head_dim, reverse=reverse,
                             scalar_decay=scalar_decay, beta_excl=beta_excl, ka_is_neg_kk=ka is None)
    args = [a for a in (r, kb, ka, kk, v, g) if a is not None]
    return pl.pallas_call(
        kern,
        grid=(B, HW // width, n_t),
        in_specs=[spec] * len(args),
        out_specs=spec,
        out_shape=jax.ShapeDtypeStruct((B, T, HW), F32),
        scratch_shapes=[pltpu.VMEM((lane_groups, V7X_MXU_DIM, V7X_MXU_DIM), F32)],
        compiler_params=_params(("parallel", "parallel", "arbitrary")),
        name="delta_scan_rev" if reverse else "delta_scan_fwd",
    )(*args)


def _matmul_kernel(*refs, nk, passes, has_bias, a_silu):
    a_ref, b_ref = refs[0], refs[1]
    bias_ref = refs[2] if has_bias else None
    o_ref = refs[2 + has_bias]
    acc_ref = refs[3 + has_bias] if nk > 1 else None
    a = a_ref[...]
    if a_silu:
        a = a.astype(F32)
        a = a * jax.nn.sigmoid(a)
    if passes == 3:
        part = _dot3(a.astype(F32), b_ref[...].astype(F32))
    else:
        part = _bdot(a, b_ref[...])

    def finish(acc):
        if has_bias:
            acc = acc + bias_ref[...]
        o_ref[...] = acc.astype(o_ref.dtype)

    if nk == 1:
        finish(part)
    else:
        k = pl.program_id(2)

        @pl.when(k == 0)
        def _():
            acc_ref[...] = part

        @pl.when(k > 0)
        def _():
            acc_ref[...] += part

        @pl.when(k == nk - 1)
        def _():
            finish(acc_ref[...])


def matmul(a, b, *, tm, tn, tk=None, bias=None, out_dtype=F32, passes=1, a_silu=False, layer=None,
           name="matmul"):
    M, K = a.shape
    N = b.shape[-1]
    tk = K if tk is None else tk
    assert M % tm == 0 and N % tn == 0 and K % tk == 0, (a.shape, b.shape, tm, tn, tk)
    nk = K // tk
    if layer is None:
        b_spec = pl.BlockSpec((tk, tn), lambda i, j, k: (k, j))
    else:
        b_spec = pl.BlockSpec((pl.Squeezed(), tk, tn), lambda i, j, k: (layer, k, j))
    in_specs = [pl.BlockSpec((tm, tk), lambda i, j, k: (i, k)), b_spec]
    args = [a, b]
    if bias is not None:
        in_specs.append(pl.BlockSpec((1, tn), lambda i, j, k: (0, j)))
        args.append(bias.reshape(1, N).astype(F32))
    kern = functools.partial(_matmul_kernel, nk=nk, passes=passes, has_bias=bias is not None, a_silu=a_silu)
    return pl.pallas_call(
        kern,
        grid=(M // tm, N // tn, nk),
        in_specs=in_specs,
        out_specs=pl.BlockSpec((tm, tn), lambda i, j, k: (i, j)),
        out_shape=jax.ShapeDtypeStruct((M, N), out_dtype),
        scratch_shapes=[pltpu.VMEM((tm, tn), F32)] if nk > 1 else [],
        compiler_params=_params(("parallel", "parallel", "arbitrary")),
        name=name,
    )(*args)


def _pack_halves(h):
    half = h.shape[1] // 2
    bits = lax.bitcast_convert_type(h.astype(BF16).astype(F32), jnp.uint32)
    return (bits[:, half:] & jnp.uint32(0xFFFF0000)) | (bits[:, :half] >> 16)


def _unpack_halves(w):
    lo = lax.bitcast_convert_type(w << 16, F32).astype(BF16)
    hi = lax.bitcast_convert_type(w & jnp.uint32(0xFFFF0000), F32).astype(BF16)
    return lo, hi


def _norm_mod_kernel(*refs, which, kinds, res_gate):
    if res_gate is None:
        x_ref, g_ref, mod_ref, *o_refs = refs
        x = x_ref[0]
    else:
        x_ref, res_ref, g_ref, mod_ref, xo_ref, *o_refs = refs
        x = x_ref[0] + mod_ref[0, res_gate:res_gate + 1, :] * res_ref[0]
        xo_ref[0] = x
    y = x * lax.rsqrt(jnp.mean(x * x, axis=-1, keepdims=True) + NORM_EPS) * g_ref[...]
    shift = mod_ref[0, 3 * which:3 * which + 1, :]
    scale = mod_ref[0, 3 * which + 1:3 * which + 2, :]
    h = y * (1.0 + scale) + shift
    for kind, o_ref in zip(kinds, o_refs):
        o_ref[0] = _pack_halves(h) if kind == "packed" else h.astype(o_ref.dtype)


def norm_mod(x, g, mod, *, which, ctx_tiles, kinds=("bf16",), residual=None, res_gate=None):
    B, T, D = x.shape
    ctx_row = mod.shape[0] - 1
    sel = lambda b, j: jnp.where(j < ctx_tiles, ctx_row, b)
    tile = lambda w: pl.BlockSpec((1, ROW_TILE, w), lambda b, j: (b, j, 0))
    shapes = {"bf16": (D, BF16), "f32": (D, F32), "packed": (D // 2, jnp.uint32)}
    kinds_out = kinds if residual is None else ("f32",) + tuple(kinds)
    outs = pl.pallas_call(
        functools.partial(_norm_mod_kernel, which=which, kinds=kinds, res_gate=res_gate),
        grid=(B, T // ROW_TILE),
        in_specs=[tile(D)] * (1 if residual is None else 2) +
                 [pl.BlockSpec((1, D), lambda b, j: (0, 0)),
                  pl.BlockSpec((1, 6, D), lambda b, j: (sel(b, j), 0, 0))],
        out_specs=[tile(shapes[k][0]) for k in kinds_out],
        out_shape=[jax.ShapeDtypeStruct((B, T, shapes[k][0]), shapes[k][1]) for k in kinds_out],
        compiler_params=_params(("parallel", "parallel")),
        name="norm_mod",
    )(*([x] if residual is None else [x, residual]), g.reshape(1, D), mod)
    return outs[0] if len(kinds_out) == 1 else outs


def _route_kernel(t_ref, wt_ref, bias_ref, idx_ref, gate_ref, cnt_ref, run_ref):
    E = wt_ref.shape[0]
    tm = t_ref.shape[0]
    per = E // N_GROUPS

    @pl.when(pl.program_id(0) == 0)
    def _():
        run_ref[...] = jnp.zeros_like(run_ref)

    th, tl = _split2(t_ref[...])
    wh, wl = _split2(wt_ref[...])
    nt = lambda a, b: lax.dot_general(a, b, (((1,), (1,)), ((), ())), preferred_element_type=F32)
    logits = nt(wh, th) + (nt(wh, tl) + nt(wl, th))
    s = jax.nn.sigmoid(logits)
    sel = s + bias_ref[...]
    srow = [s[e:e + 1, :] for e in range(E)]
    row = [sel[e:e + 1, :] for e in range(E)]
    best_g, best_i = None, None
    for gq in range(N_GROUPS):
        a, b, c, d = row[gq * per:(gq + 1) * per]
        m1, n1, m2, n2 = jnp.maximum(a, b), jnp.minimum(a, b), jnp.maximum(c, d), jnp.minimum(c, d)
        score = jnp.maximum(m1, m2) + jnp.maximum(jnp.minimum(m1, m2), jnp.maximum(n1, n2))
        if gq == 0:
            best_g, best_i = score, jnp.zeros_like(score, dtype=jnp.int32)
        else:
            better = score > best_g
            best_g = jnp.where(better, score, best_g)
            best_i = jnp.where(better, gq, best_i)
    masked = [jnp.where(best_i == e // per, row[e], NEG_INF) for e in range(E)]

    def argmax_first(vals):
        top, idx = vals[0], jnp.zeros_like(best_i)
        for e in range(1, E):
            better = vals[e] > top
            top = jnp.where(better, vals[e], top)
            idx = jnp.where(better, e, idx)
        return idx

    i1 = argmax_first(masked)
    i2 = argmax_first([jnp.where(i1 == e, -jnp.inf, masked[e]) for e in range(E)])
    w1 = sum(jnp.where(i1 == e, srow[e], 0.0) for e in range(E))
    w2 = sum(jnp.where(i2 == e, srow[e], 0.0) for e in range(E))
    tot = w1 + w2
    chosen = jnp.concatenate([jnp.where((i1 == e) | (i2 == e), 1.0, 0.0) for e in range(E)], axis=0)
    earlier = lax.broadcasted_iota(jnp.int32, (tm, tm), 0) < lax.broadcasted_iota(jnp.int32, (tm, tm), 1)
    pos = run_ref[:, 0:1] + _bdot(chosen, earlier.astype(BF16))
    run_ref[...] = run_ref[...] + jnp.sum(chosen, axis=1, keepdims=True)
    p1 = sum(jnp.where(i1 == e, pos[e:e + 1, :], 0.0) for e in range(E)).astype(jnp.int32)
    p2 = sum(jnp.where(i2 == e, pos[e:e + 1, :], 0.0) for e in range(E)).astype(jnp.int32)
    zi = jnp.zeros((V7X_SUBLANES - 4, tm), jnp.int32)
    idx_ref[...] = jnp.concatenate([i1, i2, p1, p2, zi], axis=0)
    zf = jnp.zeros((V7X_SUBLANES - 2, tm), F32)
    gate_ref[...] = jnp.concatenate([w1 / tot, w2 / tot, zf], axis=0)
    cnt_ref[...] = run_ref[...]


def moe_route(t, router_w, router_bias, *, tm=512):
    n, D = t.shape
    E = router_w.shape[1]
    tm = min(tm, n)
    row8 = pl.BlockSpec((V7X_SUBLANES, tm), lambda i: (0, i))
    return pl.pallas_call(
        _route_kernel,
        grid=(n // tm,),
        in_specs=[pl.BlockSpec((tm, D), lambda i: (i, 0)),
                  pl.BlockSpec((E, D), lambda i: (0, 0)),
                  pl.BlockSpec((E, 1), lambda i: (0, 0))],
        out_specs=[row8, row8, pl.BlockSpec((E, V7X_LANES), lambda i: (0, 0))],
        out_shape=[jax.ShapeDtypeStruct((V7X_SUBLANES, n), jnp.int32),
                   jax.ShapeDtypeStruct((V7X_SUBLANES, n), F32),
                   jax.ShapeDtypeStruct((E, V7X_LANES), F32)],
        scratch_shapes=[pltpu.VMEM((E, V7X_LANES), F32)],
        compiler_params=_params(("arbitrary",)),
        name="moe_route",
    )(t, router_w.T, router_bias.reshape(E, 1))


MOE_TILE = 1152
COPY_ROWS = 256


def _dispatch_kernel(slot_ref, x_ref, xg_ref, sem):
    n = slot_ref.shape[0] // 2
    base = pl.program_id(0) * COPY_ROWS

    def copy(r, k):
        return pltpu.make_async_copy(x_ref.at[pl.ds(r, 1)], xg_ref.at[pl.ds(slot_ref[k * n + base + r], 1)], sem)

    def start(r, carry):
        copy(r, 0).start(priority=0)
        copy(r, 1).start(priority=1)
        return carry

    def wait(r, carry):
        copy(r, 0).wait()
        copy(r, 1).wait()
        return carry

    lax.fori_loop(0, COPY_ROWS, start, 0)
    lax.fori_loop(0, COPY_ROWS, wait, 0)


def moe_dispatch(slots, x_words, n_rows):
    n, W = x_words.shape
    return pl.pallas_call(
        _dispatch_kernel,
        grid_spec=pltpu.PrefetchScalarGridSpec(
            num_scalar_prefetch=1,
            grid=(n // COPY_ROWS,),
            in_specs=[pl.BlockSpec((COPY_ROWS, W), lambda i, s: (i, 0))],
            out_specs=pl.BlockSpec(memory_space=pl.ANY),
            scratch_shapes=[pltpu.SemaphoreType.DMA(())],
        ),
        out_shape=jax.ShapeDtypeStruct((n_rows, W), x_words.dtype),
        compiler_params=_params(("arbitrary",)),
        name="moe_dispatch",
    )(slots, x_words)


def _moe_up_kernel(te_ref, rows_ref, blk_ref, x_ref, wg_ref, wu_ref, o_ref, xs_ref):
    rows = rows_ref[pl.program_id(0)]
    half = x_ref.shape[1]

    @pl.when((rows > 0) & (pl.program_id(1) == 0))
    def _():
        valid = lax.broadcasted_iota(jnp.int32, x_ref.shape, 0) < rows
        lo, hi = _unpack_halves(jnp.where(valid, x_ref[...], jnp.uint32(0)))
        xs_ref[:, :half] = lo
        xs_ref[:, half:] = hi

    @pl.when(rows > 0)
    def _():
        x = xs_ref[...]
        hg = jnp.dot(x, wg_ref[...].astype(BF16), preferred_element_type=F32)
        hu = jnp.dot(x, wu_ref[...].astype(BF16), preferred_element_type=F32)
        o_ref[...] = (hg * jax.nn.sigmoid(hg) * hu).astype(o_ref.dtype)


def _moe_down_kernel(te_ref, rows_ref, blk_ref, a_ref, wd_ref, o_ref):
    @pl.when(rows_ref[pl.program_id(0)] > 0)
    def _():
        o_ref[...] = _bdot(a_ref[...], wd_ref[...])


def moe_experts(xg, w_gate, w_up, w_down, tile_expert, tile_rows, tile_block, *, layer, tm, tf=256, tn=1024):
    _, E, D, F = w_gate.shape
    nt = tile_expert.shape[0]
    sq = pl.Squeezed()
    nf, nc = F // tf, D // tn
    hold = lambda i, last, tr, j: jnp.where(tr[j] > 0, i, last)
    act = pl.pallas_call(
        _moe_up_kernel,
        grid_spec=pltpu.PrefetchScalarGridSpec(
            num_scalar_prefetch=3,
            grid=(nt, nf),
            in_specs=[pl.BlockSpec((tm, D // 2), lambda j, f, te, tr, tb: (tb[j], 0)),
                      pl.BlockSpec((sq, sq, D, tf), lambda j, f, te, tr, tb: (layer, te[j], 0, hold(f, nf - 1, tr, j))),
                      pl.BlockSpec((sq, sq, D, tf), lambda j, f, te, tr, tb: (layer, te[j], 0, hold(f, nf - 1, tr, j)))],
            out_specs=pl.BlockSpec((tm, tf), lambda j, f, te, tr, tb: (tb[j], hold(f, nf - 1, tr, j))),
            scratch_shapes=[pltpu.VMEM((tm, D), BF16)],
        ),
        out_shape=jax.ShapeDtypeStruct((nt * tm, F), BF16),
        compiler_params=_params(("arbitrary", "arbitrary")),
        name="moe_up",
    )(tile_expert, tile_rows, tile_block, xg, w_gate, w_up)
    return pl.pallas_call(
        _moe_down_kernel,
        grid_spec=pltpu.PrefetchScalarGridSpec(
            num_scalar_prefetch=3,
            grid=(nt, nc),
            in_specs=[pl.BlockSpec((tm, F), lambda j, c, te, tr, tb: (tb[j], 0)),
                      pl.BlockSpec((sq, sq, F, tn), lambda j, c, te, tr, tb: (layer, te[j], 0, hold(c, nc - 1, tr, j)))],
            out_specs=pl.BlockSpec((tm, tn), lambda j, c, te, tr, tb: (tb[j], hold(c, nc - 1, tr, j))),
        ),
        out_shape=jax.ShapeDtypeStruct((nt * tm, D), F32),
        compiler_params=_params(("arbitrary", "arbitrary")),
        name="moe_down",
    )(tile_expert, tile_rows, tile_block, act, w_down)


def _combine_kernel(slot_ref, y_ref, gate_ref, x_ref, mod_ref, *rest, res_gate, has_final):
    fin_ref = rest[0] if has_final else None
    o_ref, ya_ref, yb_ref, sem = rest[has_final:]
    n = slot_ref.shape[0] // 2
    base = pl.program_id(0) * COPY_ROWS

    def copy(r, k):
        dst = ya_ref if k == 0 else yb_ref
        return pltpu.make_async_copy(y_ref.at[pl.ds(slot_ref[k * n + base + r], 1)], dst.at[pl.ds(r, 1)], sem)

    def start(r, carry):
        copy(r, 0).start(priority=0)
        copy(r, 1).start(priority=1)
        return carry

    def wait(r, carry):
        copy(r, 0).wait()
        copy(r, 1).wait()
        return carry

    lax.fori_loop(0, COPY_ROWS, start, 0)
    lax.fori_loop(0, COPY_ROWS, wait, 0)
    g = gate_ref[...].T
    out = x_ref[...] + mod_ref[0, res_gate:res_gate + 1, :] * (ya_ref[...] * g[:, 0:1] + yb_ref[...] * g[:, 1:2])
    if fin_ref is not None:
        out = out * lax.rsqrt(jnp.mean(out * out, axis=-1, keepdims=True) + NORM_EPS) * fin_ref[...]
    o_ref[...] = out


def moe_combine(slots, y, gates, x, mod, *, res_gate, tiles_per_batch, ctx_tiles, final_g=None):
    n, D = x.shape
    ctx_row = mod.shape[0] - 1
    sel = lambda i: jnp.where(i % tiles_per_batch < ctx_tiles, ctx_row, i // tiles_per_batch)
    in_specs = [pl.BlockSpec(memory_space=pl.ANY),
                pl.BlockSpec((V7X_SUBLANES, COPY_ROWS), lambda i, s: (0, i)),
                pl.BlockSpec((COPY_ROWS, D), lambda i, s: (i, 0)),
                pl.BlockSpec((1, 6, D), lambda i, s: (sel(i), 0, 0))]
    args = [slots, y, gates, x, mod]
    if final_g is not None:
        in_specs.append(pl.BlockSpec((1, D), lambda i, s: (0, 0)))
        args.append(final_g.reshape(1, D))
    return pl.pallas_call(
        functools.partial(_combine_kernel, res_gate=res_gate, has_final=final_g is not None),
        grid_spec=pltpu.PrefetchScalarGridSpec(
            num_scalar_prefetch=1,
            grid=(n // COPY_ROWS,),
            in_specs=in_specs,
            out_specs=pl.BlockSpec((COPY_ROWS, D), lambda i, s: (i, 0)),
            scratch_shapes=[pltpu.VMEM((COPY_ROWS, D), F32), pltpu.VMEM((COPY_ROWS, D), F32),
                            pltpu.SemaphoreType.DMA(())],
        ),
        out_shape=jax.ShapeDtypeStruct((n, D), F32),
        compiler_params=_params(("arbitrary",)),
        name="moe_combine",
    )(*args)


def _na_kernel(rpb_ref, q_ref, k_ref, v_ref, kc_ref, vc_ref, o_ref, tb_ref):
    h = pl.program_id(1)
    GW, HD = GRID_W, V7X_LANES
    n_rows = q_ref.shape[1] // GW
    wr = NA_WIN_R
    n_dc = 2 * NA_WIN_C - 1
    qcol = lax.broadcasted_iota(jnp.int32, (GW, 2 * GW), 0)
    lane = lax.broadcasted_iota(jnp.int32, (GW, 2 * GW), 1)
    kcol = lane & (GW - 1)
    upper = lane >= GW
    dci = jnp.clip(kcol - qcol + NA_WIN_C - 1, 0, n_dc - 1)
    cstart = jnp.clip(qcol - NA_WIN_C // 2, 0, GW - NA_WIN_C)
    col_ok = (kcol >= cstart) & (kcol < cstart + NA_WIN_C)
    for dr in range(2 * wr - 2):
        acc = jnp.zeros((GW, 2 * GW), F32)
        for j in range(n_dc):
            lo = rpb_ref[h, dr * n_dc + j]
            hi = rpb_ref[h, (dr + 1) * n_dc + j]
            acc = jnp.where(dci == j, jnp.where(upper, hi, lo), acc)
        tb_ref[dr] = acc
    ok4 = jnp.concatenate([col_ok] * (wr // 2), axis=1)
    kc = kc_ref[0]
    vc = vc_ref[0]
    scale = HD ** -0.5

    def body(i, carry):
        rr = [i * NA_LOCK + u for u in range(NA_LOCK)]
        rs = [jnp.clip(r - wr // 2, 0, n_rows - wr) for r in rr]
        qrow = [pl.ds(pl.multiple_of(r * GW, GW), GW) for r in rr]
        krow = [pl.ds(pl.multiple_of(s * GW, GW), wr * GW) for s in rs]
        q = [q_ref[0, qr, :] * scale for qr in qrow]
        s_lat = [_bdot_nt(a, k_ref[0, kr, :]) for a, kr in zip(q, krow)]
        s_ctx = [_bdot_nt(a, kc) for a in q]
        bias = [jnp.concatenate([tb_ref[s - r + wr - 1 + 2 * w] for w in range(wr // 2)], axis=1)
                for r, s in zip(rr, rs)]
        s_lat = [jnp.where(ok4, a + b, NEG_INF) for a, b in zip(s_lat, bias)]
        m = [jnp.maximum(jnp.max(a, axis=-1, keepdims=True), jnp.max(b, axis=-1, keepdims=True))
             for a, b in zip(s_lat, s_ctx)]
        p_lat = [jnp.exp(a - c) for a, c in zip(s_lat, m)]
        p_ctx = [jnp.exp(a - c) for a, c in zip(s_ctx, m)]
        den = [jnp.sum(a, axis=-1, keepdims=True) + jnp.sum(b, axis=-1, keepdims=True)
               for a, b in zip(p_lat, p_ctx)]
        o_lat = [_bdot(a, v_ref[0, kr, :]) for a, kr in zip(p_lat, krow)]
        o_ctx = [_bdot(a, vc) for a in p_ctx]
        for qr, a, b, d in zip(qrow, o_lat, o_ctx, den):
            o_ref[0, qr, :] = (a + b) / d
        return carry

    lax.fori_loop(0, n_rows // NA_LOCK, body, 0)


def neighbourhood_attention(pl_lat, kv_ctx, rpb):
    B, S, _ = pl_lat.shape
    Lc = kv_ctx.shape[1]
    H, HD = NA_HEADS, V7X_LANES
    lat = lambda off: pl.BlockSpec((1, S, HD), lambda b, h: (b, 0, off + h))
    cx = lambda off: pl.BlockSpec((1, Lc, HD), lambda b, h: (b, 0, off + h))
    return pl.pallas_call(
        _na_kernel,
        grid=(B, H),
        in_specs=[pl.BlockSpec(memory_space=pltpu.SMEM), lat(0), lat(H), lat(2 * H), cx(0), cx(H)],
        out_specs=pl.BlockSpec((1, S, HD), lambda b, h: (b, 0, h)),
        out_shape=jax.ShapeDtypeStruct((B, S, H * HD), F32),
        scratch_shapes=[pltpu.VMEM((2 * NA_WIN_R - 2, GRID_W, 2 * GRID_W), F32)],
        compiler_params=_params(("parallel", "parallel")),
        name="neighbourhood_attention",
    )(rpb.reshape(H, -1), pl_lat, pl_lat, pl_lat, kv_ctx, kv_ctx)


DFT_N1 = 64
DFT_N2 = 128
DFT_SUB = 8
DFT_LOCK = 4


@functools.lru_cache(maxsize=None)
def _dft_constants():
    n1, n2 = DFT_N1, DFT_N2
    n = n1 * n2
    a = np.arange(n1)
    f1 = np.exp(-2j * np.pi * np.outer(a, a) / n1)
    half = f1[:, :n1 // 2]
    eye = np.eye(DFT_SUB)
    fwd = np.kron(half, eye)
    inv = np.kron(np.conj(f1)[:n1 // 2, :], eye) / n
    b = np.arange(n2)
    f2 = np.exp(-2j * np.pi * np.outer(b, b) / n2)
    tw = np.exp(-2j * np.pi * np.outer(a, b) / n)
    m = f2[None, :, :] * tw[:, None, :]
    mt = np.transpose(m, (0, 2, 1))
    c = lambda z: (np.ascontiguousarray(z.real, dtype=np.float32), np.ascontiguousarray(z.imag, dtype=np.float32))
    return c(fwd) + c(inv) + c(m) + c(mt)


def _hyena_conv_kernel(hf_ref, hb_ref, z_ref, fwr_ref, fwi_ref, ivr_ref, ivi_ref,
                       mr_ref, mi_ref, mtr_ref, mti_ref, o_ref, ar_ref, ai_ref, kr_ref, ki_ref):
    L, TC = hf_ref.shape
    N1, N2, SUB = DFT_N1, DFT_N2, DFT_SUB
    nblk = N2 // SUB

    def stage1(load):
        def blk(nb, carry):
            off = pl.multiple_of(nb * SUB, SUB)
            xs = jnp.concatenate([load(n1 * N2 + off) for n1 in range(N1 // 2)], axis=0)
            yr = _bdot(fwr_ref[...], xs)
            yi = _bdot(fwi_ref[...], xs)
            for k1 in range(N1):
                ar_ref[pl.ds(k1 * N2 + off, SUB), :] = yr[k1 * SUB:(k1 + 1) * SUB]
                ai_ref[pl.ds(k1 * N2 + off, SUB), :] = yi[k1 * SUB:(k1 + 1) * SUB]
            return carry

        lax.fori_loop(0, nblk, blk, 0)

    def cmul_many(m_r, m_i, x_r, x_i, conj):
        rr = [_bdot(a, b) for a, b in zip(m_r, x_r)]
        ii = [_bdot(a, b) for a, b in zip(m_i, x_i)]
        ri = [_bdot(a, b) for a, b in zip(m_r, x_i)]
        ir = [_bdot(a, b) for a, b in zip(m_i, x_r)]
        if conj:
            return [a + b for a, b in zip(rr, ii)], [a - b for a, b in zip(ri, ir)]
        return [a - b for a, b in zip(rr, ii)], [a + b for a, b in zip(ri, ir)]

    def spectrum(i):
        k1s = [i * DFT_LOCK + u for u in range(DFT_LOCK)]
        rows = [pl.ds(pl.multiple_of(k1 * N2, N2), N2) for k1 in k1s]
        x_r, x_i = cmul_many([mr_ref[k1] for k1 in k1s], [mi_ref[k1] for k1 in k1s],
                             [ar_ref[r, :] for r in rows], [ai_ref[r, :] for r in rows], conj=False)
        return k1s, rows, x_r, x_i

    stage1(lambda r0: hf_ref[pl.ds(r0, SUB), :])

    def filt_f(i, carry):
        _, rows, x_r, x_i = spectrum(i)
        for r, a, b in zip(rows, x_r, x_i):
            kr_ref[r, :] = a
            ki_ref[r, :] = b
        return carry

    lax.fori_loop(0, N1 // DFT_LOCK, filt_f, 0)
    sub_iota = lax.broadcasted_iota(jnp.int32, (SUB, TC), 0)
    stage1(lambda r0: jnp.where(sub_iota + r0 == 0, 0.0, hb_ref[pl.ds(r0, SUB), :]))

    def filt_b(i, carry):
        _, rows, x_r, x_i = spectrum(i)
        for r, a, b in zip(rows, x_r, x_i):
            kr_ref[r, :] += a
            ki_ref[r, :] -= b
        return carry

    lax.fori_loop(0, N1 // DFT_LOCK, filt_b, 0)

    for b in range(z_ref.shape[0]):
        stage1(lambda r0: z_ref[b, pl.ds(r0, SUB), :])

        def conv(i, carry):
            k1s, rows, x_r, x_i = spectrum(i)
            f_r, f_i = [kr_ref[r, :] for r in rows], [ki_ref[r, :] for r in rows]
            y_r = [a * c - b_ * d for a, b_, c, d in zip(x_r, x_i, f_r, f_i)]
            y_i = [a * d + b_ * c for a, b_, c, d in zip(x_r, x_i, f_r, f_i)]
            o_r, o_i = cmul_many([mtr_ref[k1] for k1 in k1s], [mti_ref[k1] for k1 in k1s], y_r, y_i, conj=True)
            for r, a, c in zip(rows, o_r, o_i):
                ar_ref[r, :] = a
                ai_ref[r, :] = c
            return carry

        lax.fori_loop(0, N1 // DFT_LOCK, conv, 0)
        def last(nb, carry):
            off = pl.multiple_of(nb * SUB, SUB)
            br = jnp.concatenate([ar_ref[pl.ds(k1 * N2 + off, SUB), :] for k1 in range(N1)], axis=0)
            bi = jnp.concatenate([ai_ref[pl.ds(k1 * N2 + off, SUB), :] for k1 in range(N1)], axis=0)
            y = _bdot(ivr_ref[...], br) - _bdot(ivi_ref[...], bi)
            for n1 in range(N1 // 2):
                o_ref[b, pl.ds(n1 * N2 + off, SUB), :] = y[n1 * SUB:(n1 + 1) * SUB]
            return carry

        lax.fori_loop(0, nblk, last, 0)


def hyena_long_conv(z, h_f, h_b, *, tc=V7X_LANES):
    B, L, C = z.shape
    assert L == DFT_N1 * DFT_N2 // 2
    consts = [jnp.asarray(a, BF16) for a in _dft_constants()]
    whole = lambda a: pl.BlockSpec(a.shape, lambda j: (0,) * a.ndim)
    col = pl.BlockSpec((L, tc), lambda j: (0, j))
    bcol = pl.BlockSpec((B, L, tc), lambda j: (0, 0, j))
    return pl.pallas_call(
        _hyena_conv_kernel,
        grid=(C // tc,),
        in_specs=[col, col, bcol] + [whole(a) for a in consts],
        out_specs=bcol,
        out_shape=jax.ShapeDtypeStruct((B, L, C), F32),
        scratch_shapes=[pltpu.VMEM((2 * L, tc), F32)] * 4,
        compiler_params=_params(("parallel",)),
        name="hyena_long_conv",
    )(h_f, h_b, z, *consts)


def _pick(n, options):
    for t in options:
        if n % t == 0:
            return t
    raise ValueError(f"no tile for {n} in {options}")


def _mm(a, w, **kw):
    tm = _pick(a.shape[0], (1088, 1024, 768, 512, 256, 8))
    tn = _pick(w.shape[1], (512, 640, 384, 256, 128))
    return matmul(a, w, tm=tm, tn=tn, **kw)


def _pad_to(a, axis, mult):
    pad = (-a.shape[axis]) % mult
    if pad == 0:
        return a
    widths = [(0, 0)] * a.ndim
    widths[axis] = (0, pad)
    return jnp.pad(a, widths)


def _silu(x):
    return x * jax.nn.sigmoid(x)


def _seq_parts(x, lc, fn):
    if lc == 0:
        return fn(x)
    return jnp.concatenate([fn(x[:, :lc]), fn(x[:, lc:])], axis=1)


def _dwconv(x, w):
    k, length = w.shape[0], x.shape[1]
    xp = jnp.pad(x, ((0, 0), (k // 2, k // 2), (0, 0)))
    return sum(xp[:, j:j + length] * w[j] for j in range(k))


def _centred_shift(p):
    prev = jnp.pad(p, ((0, 0), (1, 0), (0, 0)))[:, :-1]
    nxt = jnp.pad(p, ((0, 0), (0, 1), (0, 0)))[:, 1:]
    return 0.5 * (prev + nxt)


def _l2norm(t):
    return t * lax.rsqrt(jnp.sum(t * t, -1, keepdims=True) + 1e-6)


def _axial_rope(t):
    length, d = t.shape[1], t.shape[-1]
    nf = d // 4
    pos = jnp.arange(length)
    inv = ROPE_BASE ** (-jnp.arange(nf, dtype=F32) / nf)

    def cs(p):
        ang = p.astype(F32)[:, None] * inv[None, :]
        return jnp.cos(ang), jnp.sin(ang)

    cr, sr = cs(pos // GRID_W)
    cc, sc = cs(pos % GRID_W)
    cos = jnp.concatenate([cr, cr, cc, cc], -1)[None, :, None, :]
    sin = jnp.concatenate([-sr, sr, -sc, sc], -1)[None, :, None, :]
    swapped = jnp.flip(t.reshape(*t.shape[:-1], 2, 2, nf), axis=-2).reshape(t.shape)
    return t * cos + swapped * sin


def _mixer_ab(h, lc, w_in, dn_conv_w, dn_a_log, dn_dt_bias, dn_norm_w,
              rw_mu, rw_w0, rw_w2, rw_a0, rw_a2, rw_g2, rw_k_k, rw_k_a, rw_r_k, rw_ln_w, rw_ln_b, w_out):
    B, T, D = h.shape
    mw = D // 2
    hd_dn = mw // DN_HEADS
    h_rw = mw // RW_HD
    p_a = 4 * mw + 4 * DN_HEADS
    h2 = h.reshape(B * T, D)
    pa = _mm(h2, _pad_to(w_in[:, :p_a], 1, V7X_LANES).astype(BF16)).reshape(B, T, -1)
    pb = _mm(h2, _pad_to(w_in[:, p_a:], 1, V7X_LANES).astype(BF16)).reshape(B, T, -1)

    qkv = _seq_parts(pa[..., :3 * mw], lc, lambda u: _silu(_dwconv(u, dn_conv_w)))
    q, k, v = (t.reshape(B, T, DN_HEADS, hd_dn) for t in jnp.split(qkv, 3, axis=-1))
    rope_lat = lambda t: jnp.concatenate([t[:, :lc], _axial_rope(t[:, lc:])], axis=1)
    q, k = rope_lat(_l2norm(q)), rope_lat(_l2norm(k))
    q = q * hd_dn ** -0.5
    z = pa[..., 3 * mw:4 * mw]
    bg = pa[..., 4 * mw:p_a].reshape(B, T, 4, DN_HEADS)
    beta = jax.nn.sigmoid(bg[:, :, :2])
    g = -jnp.exp(dn_a_log) * jax.nn.softplus(bg[:, :, 2:] + dn_dt_bias)
    flat = lambda t: t.reshape(B, T, mw)
    o = 0.0
    for d in range(2):
        bk = k * beta[:, :, d, :, None]
        gd = jnp.broadcast_to(g[:, :, d, :, None], k.shape)
        o = o + delta_scan(flat(q), flat(k), None, flat(bk), flat(v), flat(gd),
                           head_dim=hd_dn, reverse=d == 1, scalar_decay=True, beta_excl=False)
    o = o.reshape(B, T, DN_HEADS, hd_dn)
    o = o * lax.rsqrt(jnp.mean(o * o, -1, keepdims=True) + NORM_EPS) * dn_norm_w
    dn = flat(o * _silu(z).reshape(o.shape))

    p_b = w_in.shape[1] - p_a
    p = pb[..., :p_b]
    p = p + (_seq_parts(p, lc, _centred_shift) - p) * rw_mu
    r, k, v = p[..., :mw], p[..., mw:2 * mw], p[..., 2 * mw:3 * mw]
    o1 = 3 * mw
    n_w, n_a = rw_w2.shape[1], rw_a2.shape[1]
    o2 = o1 + 2 * n_w
    o3 = o2 + 2 * n_a
    wl = p[..., o1:o2].reshape(B * T, 2, n_w)
    al = p[..., o2:o3].reshape(B * T, 2, n_a)
    gl = p[..., o3:].reshape(B * T, -1)
    gate = _mm(_pad_to(jax.nn.sigmoid(gl), 1, V7X_LANES), _pad_to(rw_g2, 0, V7X_LANES)).reshape(B, T, mw)
    heads = lambda t: t.reshape(B, T, h_rw, RW_HD)
    kk = flat(_l2norm(heads(k * rw_k_k)))
    y = 0.0
    for d in range(2):
        w = -jax.nn.softplus(-(rw_w0[d] + _mm(jnp.tanh(wl[:, d]), rw_w2[d]).reshape(B, T, mw))) - 0.5
        a = jax.nn.sigmoid(rw_a0[d] + _mm(al[:, d], rw_a2[d]).reshape(B, T, mw))
        kd = k * (1.0 + (a - 1.0) * rw_k_a)
        y = y + delta_scan(r, kk, -(kk * a), kd, v, -jnp.exp(w),
                           head_dim=RW_HD, reverse=d == 1, scalar_decay=False, beta_excl=True)
    y = heads(y)
    mu = jnp.mean(y, -1, keepdims=True)
    var = jnp.mean(jnp.square(y - mu), -1, keepdims=True)
    yn = flat((y - mu) * lax.rsqrt(var + RW_LN_EPS)) * rw_ln_w + rw_ln_b
    bonus = flat(jnp.sum(heads(r) * heads(k) * rw_r_k, -1, keepdims=True) * heads(v))
    rw = (yn + bonus) * gate

    cat = jnp.concatenate([dn, rw], -1).astype(BF16).reshape(B * T, D)
    return _mm(cat, w_out.astype(BF16)).reshape(B, T, D)


def _hyena_filters(length, ch, w1, b1, freq, w2, b2, w3, b3, w4):
    t = jnp.linspace(0.0, 1.0, length, dtype=F32)
    bands = (HY_EMB - 1) // 2
    wpos = 2 * math.pi * jnp.arange(length, dtype=F32) / length
    fb = jnp.linspace(1e-4, bands - 1, bands, dtype=F32)
    ang = wpos[:, None] * fb[None, :]
    z = jnp.concatenate([t[:, None], jnp.cos(ang), -jnp.sin(ang)], axis=-1)
    dense = lambda u, w: _mm(_pad_to(u, 1, V7X_LANES), _pad_to(_pad_to(w, 0, V7X_LANES), 1, V7X_LANES),
                             passes=3)[:, :w.shape[1]]
    h = jnp.sin(freq * (dense(z, w1) + b1))
    h = jnp.sin(freq * (dense(h, w2) + b2))
    h = jnp.sin(freq * (dense(h, w3) + b3))
    h = dense(h, w4).reshape(length, 2, ch)
    deltas = jnp.abs(jnp.linspace(HY_MIN_DECAY, HY_MAX_DECAY, ch, dtype=F32))
    window = jnp.exp(-t[:, None] * deltas[None, :])
    return h[:, 0] * window, h[:, 1] * window


def _mixer_cd(hc, hl, w_in, na_rpb, hy_conv_w, hy_conv_b, filt, hy_skip, w_out):
    B, S, D = hl.shape
    lc = hc.shape[1]
    mw = D // 2
    w_bf = w_in.astype(BF16)
    pl_ = _mm(hl.reshape(B * S, D), w_bf).reshape(B, S, -1)
    kv_c = _mm(hc.reshape(B * lc, D), w_bf[:, mw:3 * mw]).reshape(B, lc, 2 * mw)
    att = neighbourhood_attention(pl_, kv_c, na_rpb)
    u = _dwconv(pl_[..., 3 * mw:], hy_conv_w) + hy_conv_b
    x0, x1, v = jnp.split(u, 3, axis=-1)
    h_f, h_b = _hyena_filters(S, mw, *filt)
    z = x1 * v
    hy = x0 * (hyena_long_conv(z, h_f, h_b) + z * hy_skip)
    cat = jnp.concatenate([att, hy], -1).astype(BF16).reshape(B * S, D)
    return _mm(cat, w_out.astype(BF16)).reshape(B, S, D)


def _moe(t_packed, t_f32, router_w, router_bias, w_gate, w_up, w_down, layer, x_res, tail, tile=MOE_TILE):
    n, D = t_f32.shape
    E = w_gate.shape[1]
    idx, gates, counts = moe_route(t_f32, router_w, router_bias)
    cnt = counts[:, 0].astype(jnp.int32)
    nt = -(-2 * n // tile) + E
    tiles = (cnt + tile - 1) // tile
    ends = jnp.cumsum(tiles)
    starts = ends - tiles
    used = ends[-1]
    j = jnp.arange(nt, dtype=jnp.int32)
    jj = jnp.minimum(j, used - 1)
    tile_expert = jnp.sum((jj[:, None] >= ends[None, :]).astype(jnp.int32), axis=1)
    tile_rows = jnp.where(j < used, jnp.clip(cnt[tile_expert] - (jj - starts[tile_expert]) * tile, 0, tile), 0)
    row0 = starts * tile
    slots = jnp.concatenate([row0[idx[0]] + idx[2], row0[idx[1]] + idx[3]]).astype(jnp.int32)
    xg = moe_dispatch(slots, t_packed, nt * tile)
    y = moe_experts(xg, w_gate, w_up, w_down, tile_expert.astype(jnp.int32), tile_rows.astype(jnp.int32), jj,
                    layer=layer, tm=tile)
    return moe_combine(slots, y, gates, x_res, **tail)


def kernel(x, c, ctx, c_ctx, ada_w, ada_b, norm1_g, norm2_g, final_g, ab_w_in, dn_conv_w, dn_a_log, dn_dt_bias, dn_norm_w, rw_mu, rw_w0, rw_w2, rw_a0, rw_a2, rw_g2, rw_k_k, rw_k_a, rw_r_k, rw_ln_w, rw_ln_b, ab_w_out, cd_w_in, na_rpb, hy_conv_w, hy_conv_b, hy_w1, hy_b1, hy_freq, hy_w2, hy_b2, hy_w3, hy_b3, hy_w4, hy_skip, cd_w_out, router_w, router_bias, moe_w_gate, moe_w_up, moe_w_down):
    B, S, D = x.shape
    lc = ctx.shape[1]
    assert ada_w.shape[0] == 2 and lc == ROW_TILE and S % ROW_TILE == 0 and B + 1 <= V7X_SUBLANES
    T = lc + S
    ctx_tiles = lc // ROW_TILE
    cond = jnp.concatenate([c, c_ctx[None], jnp.zeros((V7X_SUBLANES - B - 1, D), F32)], axis=0)
    mods = [matmul(cond, ada_w, tm=V7X_SUBLANES, tn=512, bias=ada_b[l], a_silu=True, layer=l,
                   name="adaln")[:B + 1].reshape(B + 1, 6, D) for l in range(2)]

    m = mods[0]
    xa = jnp.concatenate([ctx, x], axis=1)
    h = norm_mod(xa, norm1_g[0], m, which=0, ctx_tiles=ctx_tiles)
    mix = _mixer_ab(h, lc, ab_w_in[0], dn_conv_w[0], dn_a_log[0], dn_dt_bias[0], dn_norm_w[0],
                    rw_mu[0], rw_w0[0], rw_w2[0], rw_a0[0], rw_a2[0], rw_g2[0], rw_k_k[0], rw_k_a[0],
                    rw_r_k[0], rw_ln_w[0], rw_ln_b[0], ab_w_out[0])
    xa, h_pk, h_f32 = norm_mod(xa, norm2_g[0], m, which=1, ctx_tiles=ctx_tiles, kinds=("packed", "f32"),
                               residual=mix, res_gate=2)
    tail = dict(mod=m, res_gate=5, tiles_per_batch=T // COPY_ROWS, ctx_tiles=ctx_tiles)
    xa = _moe(h_pk.reshape(B * T, D // 2), h_f32.reshape(B * T, D), router_w, router_bias,
              moe_w_gate, moe_w_up, moe_w_down, 0, xa.reshape(B * T, D), tail).reshape(B, T, D)

    m = mods[1]
    cx, xl = xa[:, :lc], xa[:, lc:]
    hl = norm_mod(xl, norm1_g[1], m, which=0, ctx_tiles=0)
    hc = norm_mod(cx, norm1_g[1], m, which=0, ctx_tiles=ctx_tiles)
    filt = (hy_w1[0], hy_b1[0], hy_freq[0], hy_w2[0], hy_b2[0], hy_w3[0], hy_b3[0], hy_w4[0])
    ml = _mixer_cd(hc, hl, cd_w_in[0], na_rpb[0], hy_conv_w[0], hy_conv_b[0], filt, hy_skip[0], cd_w_out[0])
    xl, h_pk, h_f32 = norm_mod(xl, norm2_g[1], m, which=1, ctx_tiles=0, kinds=("packed", "f32"),
                               residual=ml, res_gate=2)
    tail = dict(mod=m, res_gate=5, tiles_per_batch=S // COPY_ROWS, ctx_tiles=0, final_g=final_g)
    return _moe(h_pk.reshape(B * S, D // 2), h_f32.reshape(B * S, D), router_w, router_bias,
                moe_w_gate, moe_w_up, moe_w_down, 1, xl.reshape(B * S, D), tail).reshape(B, S, D)
```
